```python
import jax, jax.numpy as jnp
from jax import lax
import numpy as np

D_MODEL = 1024
BATCH = 4
SEQ = 8192
DEPTH = 1

D_RNN = 1024
RNN_BLOCKS = 8
RNN_BW = D_RNN // RNN_BLOCKS
CONV_W = 4
LRU_C = 8.0
N_HEADS = 8
HEAD_DIM = 128
D_ATTN = N_HEADS * HEAD_DIM
MOBA_BLOCK = 256
MOBA_TOPK = 3
Q_CHUNK = 32
N_BRANCH = 2
D_FF = 2816
FFN_CONV_W = 3

D_IN = 2 * D_RNN + 3 * D_ATTN + N_BRANCH * D_MODEL
EPS = 1e-6
NEG = -1e30

kernel_name = "hybrid_rglru_moba_convffn_block"


def rmsnorm(x, g):
    xf = x.astype(jnp.float32)
    y = xf * lax.rsqrt(jnp.mean(xf * xf, axis=-1, keepdims=True) + EPS)
    return (y * g.astype(jnp.float32)).astype(x.dtype)


def causal_dwconv(x, w, b):
    width = w.shape[0]
    s = x.shape[1]
    xp = jnp.pad(x, ((0, 0), (width - 1, 0), (0, 0)))
    y = b
    for k in range(width):
        y = y + xp[:, k:k + s] * w[k]
    return y


def rg_lru(x, w_r, b_r, w_i, b_i, lam):
    bsz, s, _ = x.shape
    xb = x.reshape(bsz, s, RNN_BLOCKS, RNN_BW)
    r = jax.nn.sigmoid(jnp.einsum('bsnc,ncd->bsnd', xb, w_r) + b_r).reshape(bsz, s, D_RNN)
    i = jax.nn.sigmoid(jnp.einsum('bsnc,ncd->bsnd', xb, w_i) + b_i).reshape(bsz, s, D_RNN)
    log_a = (-LRU_C * r.astype(jnp.float32)) * jax.nn.softplus(-lam.astype(jnp.float32))
    a = jnp.exp(log_a)
    mult = jnp.sqrt(-jnp.expm1(2.0 * log_a))
    bx = mult * (i * x).astype(jnp.float32)

    def combine(c1, c2):
        a1, b1 = c1
        a2, b2 = c2
        return a1 * a2, a2 * b1 + b2

    _, h = lax.associative_scan(combine, (a, bx), axis=1)
    return h.astype(x.dtype)


def moba_attention(q, k, v):
    bsz, nh, s, hd = q.shape
    nb = -(-s // MOBA_BLOCK)
    s_pad = nb * MOBA_BLOCK
    pad = ((0, 0), (0, 0), (0, s_pad - s), (0, 0))
    q = jnp.pad(q, pad)
    k = jnp.pad(k, pad)
    v = jnp.pad(v, pad)
    kb = k.reshape(bsz, nh, nb, MOBA_BLOCK, hd)
    vb = v.reshape(bsz, nh, nb, MOBA_BLOCK, hd)
    k_mean = jnp.mean(kb.astype(jnp.float32), axis=3)
    n_sel = min(MOBA_TOPK, nb)
    n_chunks = s_pad // Q_CHUNK
    scale = HEAD_DIM ** -0.5
    qc = q.reshape(bsz, nh, n_chunks, Q_CHUNK, hd).transpose(2, 0, 1, 3, 4)
    gather = jax.vmap(jax.vmap(lambda t, ix: t[ix]))

    def chunk_fn(args):
        ci, qi = args
        q_pos = ci * Q_CHUNK + jnp.arange(Q_CHUNK)
        own = (ci * Q_CHUNK) // MOBA_BLOCK
        gate = jnp.einsum('bhqd,bhnd->bhqn', qi.astype(jnp.float32), k_mean)
        gate = jnp.where(jnp.arange(nb) < own, gate, NEG)
        _, idx = lax.top_k(gate, n_sel)
        slot_ok = jnp.arange(n_sel) < own
        k_sel = gather(kb, idx)
        v_sel = gather(vb, idx)
        k_own = lax.dynamic_slice_in_dim(k, own * MOBA_BLOCK, MOBA_BLOCK, axis=2)
        v_own = lax.dynamic_slice_in_dim(v, own * MOBA_BLOCK, MOBA_BLOCK, axis=2)
        s_sel = jnp.einsum('bhqd,bhqnkd->bhqnk', qi, k_sel).astype(jnp.float32) * scale
        s_sel = jnp.where(slot_ok[:, None], s_sel, NEG).reshape(bsz, nh, Q_CHUNK, n_sel * MOBA_BLOCK)
        s_own = jnp.einsum('bhqd,bhkd->bhqk', qi, k_own).astype(jnp.float32) * scale
        k_pos = own * MOBA_BLOCK + jnp.arange(MOBA_BLOCK)
        s_own = jnp.where(k_pos[None, :] <= q_pos[:, None], s_own, NEG)
        p = jax.nn.softmax(jnp.concatenate([s_sel, s_own], axis=-1), axis=-1).astype(v.dtype)
        p_sel = p[..., :n_sel * MOBA_BLOCK].reshape(bsz, nh, Q_CHUNK, n_sel, MOBA_BLOCK)
        p_own = p[..., n_sel * MOBA_BLOCK:]
        return (jnp.einsum('bhqnk,bhqnkd->bhqd', p_sel, v_sel)
                + jnp.einsum('bhqk,bhkd->bhqd', p_own, v_own))

    o = lax.map(chunk_fn, (jnp.arange(n_chunks), qc))
    o = o.transpose(1, 2, 0, 3, 4).reshape(bsz, nh, s_pad, hd)[:, :, :s]
    return o


def setup_inputs(seed: int = 0) -> dict:
    key = jax.random.key(seed)
    ks = jax.random.split(key, 24)
    f32 = jnp.float32
    L = DEPTH

    def nrm(k, shape, scale):
        return jax.random.normal(k, shape, f32) * scale

    a0 = jax.random.uniform(ks[9], (L, D_RNN), f32, 0.9, 0.999)
    s0 = a0 ** (1.0 / LRU_C)
    lru_lambda = jnp.log(s0) - jnp.log1p(-s0)
    return {
        "x": jax.random.normal(ks[0], (BATCH, SEQ, D_MODEL), f32),
        "norm1_g": 1.0 + nrm(ks[1], (L, D_MODEL), 0.02),
        "w_in": nrm(ks[2], (L, D_MODEL, D_IN), D_MODEL ** -0.5),
        "conv_w": nrm(ks[3], (L, CONV_W, D_RNN), CONV_W ** -0.5),
        "conv_b": nrm(ks[4], (L, D_RNN), 0.01),
        "w_r": nrm(ks[5], (L, RNN_BLOCKS, RNN_BW, RNN_BW), RNN_BW ** -0.5),
        "b_r": nrm(ks[6], (L, RNN_BLOCKS, RNN_BW), 0.01),
        "w_i": nrm(ks[7], (L, RNN_BLOCKS, RNN_BW, RNN_BW), RNN_BW ** -0.5),
        "b_i": nrm(ks[8], (L, RNN_BLOCKS, RNN_BW), 0.01),
        "lru_lambda": lru_lambda,
        "q_norm_g": 1.0 + nrm(ks[10], (L, HEAD_DIM), 0.02),
        "k_norm_g": 1.0 + nrm(ks[11], (L, HEAD_DIM), 0.02),
        "w_proj_rnn": nrm(ks[12], (L, D_RNN, D_MODEL), D_RNN ** -0.5),
        "w_proj_attn": nrm(ks[13], (L, D_ATTN, D_MODEL), D_ATTN ** -0.5),
        "w_out": nrm(ks[14], (L, D_MODEL, D_MODEL), D_MODEL ** -0.5),
        "norm2_g": 1.0 + nrm(ks[15], (L, D_MODEL), 0.02),
        "w_up": nrm(ks[16], (L, D_MODEL, D_FF), D_MODEL ** -0.5),
        "w_gate": nrm(ks[17], (L, D_MODEL, D_FF), D_MODEL ** -0.5),
        "ffn_conv_w": nrm(ks[18], (L, FFN_CONV_W, D_FF), FFN_CONV_W ** -0.5),
        "ffn_conv_b": nrm(ks[19], (L, D_FF), 0.01),
        "w_down": nrm(ks[20], (L, D_FF, D_MODEL), D_FF ** -0.5),
    }


def reference(x, norm1_g, w_in, conv_w, conv_b, w_r, b_r, w_i, b_i, lru_lambda,
              q_norm_g, k_norm_g, w_proj_rnn, w_proj_attn, w_out, norm2_g,
              w_up, w_gate, ffn_conv_w, ffn_conv_b, w_down):
    bsz, s, _ = x.shape
    cuts = np.cumsum([D_RNN, D_RNN, D_ATTN, D_ATTN, D_ATTN])
    for l in range(DEPTH):
        h = rmsnorm(x, norm1_g[l])
        u = jnp.einsum('bsd,de->bse', h, w_in[l])
        x_rnn, g_rnn, q, k, v, g_br = jnp.split(u, list(cuts), axis=-1)
        xa = causal_dwconv(x_rnn, conv_w[l], conv_b[l])
        ya = rg_lru(xa, w_r[l], b_r[l], w_i[l], b_i[l], lru_lambda[l]) * jax.nn.gelu(g_rnn)
        q = rmsnorm(q.reshape(bsz, s, N_HEADS, HEAD_DIM), q_norm_g[l]).transpose(0, 2, 1, 3)
        k = rmsnorm(k.reshape(bsz, s, N_HEADS, HEAD_DIM), k_norm_g[l]).transpose(0, 2, 1, 3)
        v = v.reshape(bsz, s, N_HEADS, HEAD_DIM).transpose(0, 2, 1, 3)
        yb = moba_attention(q, k, v).transpose(0, 2, 1, 3).reshape(bsz, s, D_ATTN)
        gates = jax.nn.sigmoid(g_br).reshape(bsz, s, N_BRANCH, D_MODEL)
        merged = (gates[:, :, 0] * jnp.einsum('bsc,cd->bsd', ya, w_proj_rnn[l])
                  + gates[:, :, 1] * jnp.einsum('bsc,cd->bsd', yb, w_proj_attn[l]))
        x = x + jnp.einsum('bsd,de->bse', merged, w_out[l])
        h = rmsnorm(x, norm2_g[l])
        up = causal_dwconv(jnp.einsum('bsd,df->bsf', h, w_up[l]), ffn_conv_w[l], ffn_conv_b[l])
        act = jax.nn.gelu(up) * jnp.einsum('bsd,df->bsf', h, w_gate[l])
        x = x + jnp.einsum('bsf,fd->bsd', act, w_down[l])
    return x
```

```python
import functools

import jax
import jax.numpy as jnp
from jax import lax
from jax.experimental import pallas as pl
from jax.experimental.pallas import tpu as pltpu

D_MODEL = 1024
D_RNN = 1024
RNN_BLOCKS = 8
RNN_BW = D_RNN // RNN_BLOCKS
CONV_W = 4
LRU_C = 8.0
N_HEADS = 8
HEAD_DIM = 128
D_ATTN = N_HEADS * HEAD_DIM
MOBA_BLOCK = 256
MOBA_TOPK = 3
D_FF = 2816
FFN_CONV_W = 3
EPS = 1e-6
NEG = -1e30

LANES = 128
SUBLANES = 8
VMEM_LIMIT = 56 * 1024 * 1024

F32 = jnp.float32
BF16 = jnp.bfloat16

C_RNN = 0
C_Q = 2 * D_RNN
C_K = C_Q + D_ATTN
C_V = C_K + D_ATTN
C_G = C_V + D_ATTN
D_IN = C_G + 2 * D_MODEL


def _rms(y, g):
    return y * lax.rsqrt(jnp.mean(y * y, axis=-1, keepdims=True) + EPS) * g


def _gelu(x):
    c = 0.7978845608028654
    return 0.5 * x * (1.0 + jnp.tanh(c * (x + 0.044715 * (x * x * x))))


def _const_spec(shape):
    nd = len(shape)
    return pl.BlockSpec(shape, lambda *_: (0,) * nd, pipeline_mode=pl.Buffered(1))


def _in_proj_kernel(x_ref, g1_ref, w_ref, qg_ref, kg_ref,
                    rnn_ref, q_ref, k_ref, v_ref, km_ref, gate_ref, *, tm, tiles_per_seq):
    hb = _rms(x_ref[...], g1_ref[...]).astype(BF16)

    def proj(c0, c1):
        return jnp.dot(hb, w_ref[:, c0:c1], preferred_element_type=F32)

    rnn_ref[...] = proj(C_RNN, C_Q)

    scale = HEAD_DIM ** -0.5
    yq = proj(C_Q, C_K)
    for h in range(N_HEADS):
        y = _rms(yq[:, h * HEAD_DIM:(h + 1) * HEAD_DIM], qg_ref[...]) * scale
        q_ref[0, h] = y.astype(BF16)

    s0 = (pl.program_id(0) % tiles_per_seq) * tm
    row = lax.broadcasted_iota(jnp.int32, (tm, LANES), 0) + s0
    lane = lax.broadcasted_iota(jnp.int32, (tm, LANES), 1)
    onehot = jnp.where(lane == row // MOBA_BLOCK, 1.0, 0.0).astype(BF16)
    yk = proj(C_K, C_V)
    for h in range(N_HEADS):
        y = _rms(yk[:, h * HEAD_DIM:(h + 1) * HEAD_DIM], kg_ref[...])
        k_ref[0, h, :, 0:HEAD_DIM] = y.astype(BF16)
        k_ref[0, h, :, HEAD_DIM:2 * HEAD_DIM] = onehot
        for blk in range(tm // MOBA_BLOCK):
            km_ref[0, blk, :, h * HEAD_DIM:(h + 1) * HEAD_DIM] = jnp.mean(
                y[blk * MOBA_BLOCK:(blk + 1) * MOBA_BLOCK], axis=0, keepdims=True)

    yv = proj(C_V, C_G)
    for h in range(N_HEADS):
        v_ref[0, h] = yv[:, h * HEAD_DIM:(h + 1) * HEAD_DIM].astype(BF16)

    gate_ref[...] = jax.nn.sigmoid(proj(C_G, D_IN))


def _in_proj(x2, g1, w_in, qg, kg, *, bsz, seq, tm):
    t = bsz * seq
    nst = seq // tm
    nblk = seq // MOBA_BLOCK
    kern = functools.partial(_in_proj_kernel, tm=tm, tiles_per_seq=nst)
    return pl.pallas_call(
        kern,
        grid=(t // tm,),
        in_specs=[
            pl.BlockSpec((tm, D_MODEL), lambda i: (i, 0)),
            _const_spec((1, D_MODEL)),
            _const_spec((D_MODEL, D_IN)),
            _const_spec((1, HEAD_DIM)),
            _const_spec((1, HEAD_DIM)),
        ],
        out_specs=[
            pl.BlockSpec((tm, 2 * D_RNN), lambda i: (i, 0)),
            pl.BlockSpec((1, N_HEADS, tm, HEAD_DIM), lambda i: (i // nst, 0, i % nst, 0)),
            pl.BlockSpec((1, N_HEADS, tm, 2 * HEAD_DIM), lambda i: (i // nst, 0, i % nst, 0)),
            pl.BlockSpec((1, N_HEADS, tm, HEAD_DIM), lambda i: (i // nst, 0, i % nst, 0)),
            pl.BlockSpec((1, tm // MOBA_BLOCK, 1, D_ATTN), lambda i: (i // nst, i % nst, 0, 0)),
            pl.BlockSpec((tm, 2 * D_MODEL), lambda i: (i, 0)),
        ],
        out_shape=[
            jax.ShapeDtypeStruct((t, 2 * D_RNN), F32),
            jax.ShapeDtypeStruct((bsz, N_HEADS, seq, HEAD_DIM), BF16),
            jax.ShapeDtypeStruct((bsz, N_HEADS, seq, 2 * HEAD_DIM), BF16),
            jax.ShapeDtypeStruct((bsz, N_HEADS, seq, HEAD_DIM), BF16),
            jax.ShapeDtypeStruct((bsz, nblk, 1, D_ATTN), F32),
            jax.ShapeDtypeStruct((t, 2 * D_MODEL), F32),
        ],
        compiler_params=pltpu.CompilerParams(
            dimension_semantics=("arbitrary",), vmem_limit_bytes=VMEM_LIMIT),
        name="in_proj",
    )(x2, g1, w_in, qg, kg)


def _shift_rows(x, d, fill, row):
    return jnp.where(row >= d, pltpu.roll(x, d, 0), fill)


def _rglru_kernel(rnn_ref, cw_ref, cb_ref, wri_ref, bri_ref, lam_ref, ya_ref, xbuf, hst, *, ts):
    s = pl.program_id(1)

    @pl.when(s == 0)
    def _():
        xbuf[...] = jnp.zeros_like(xbuf)
        hst[...] = jnp.zeros_like(hst)

    xbuf[0:SUBLANES, :] = xbuf[ts:ts + SUBLANES, :]
    xbuf[SUBLANES:SUBLANES + ts, :] = rnn_ref[0, :, 0:D_RNN]

    row = lax.broadcasted_iota(jnp.int32, (ts, RNN_BW), 0)
    for n in range(RNN_BLOCKS):
        cols = slice(n * RNN_BW, (n + 1) * RNN_BW)
        xa = cb_ref[:, cols]
        for k in range(CONV_W):
            off = SUBLANES - (CONV_W - 1) + k
            xa = xa + xbuf[off:off + ts, cols] * cw_ref[k:k + 1, cols]
        z = jnp.dot(xa.astype(BF16), wri_ref[n], preferred_element_type=F32) + bri_ref[n]
        r = jax.nn.sigmoid(z[:, 0:RNN_BW])
        gi = jax.nn.sigmoid(z[:, RNN_BW:2 * RNN_BW])
        log_a = (-LRU_C * r) * jax.nn.softplus(-lam_ref[:, cols])
        a = jnp.exp(log_a)
        b = jnp.sqrt(-jnp.tanh(log_a) * (a * a + 1.0)) * (gi * xa)
        d = 1
        while d < ts:
            a_sh = _shift_rows(a, d, 1.0, row)
            b_sh = _shift_rows(b, d, 0.0, row)
            b = a * b_sh + b
            a = a * a_sh
            d *= 2
        h = b + a * hst[0:1, cols]
        hst[0:1, cols] = h[ts - 1:ts, :]
        ya_ref[0, :, cols] = (h * _gelu(rnn_ref[0, :, D_RNN + n * RNN_BW:D_RNN + (n + 1) * RNN_BW])).astype(BF16)


def _rglru(rnn3, cw, cb, wri, bri, lam, *, ts):
    bsz, seq, _ = rnn3.shape
    kern = functools.partial(_rglru_kernel, ts=ts)
    return pl.pallas_call(
        kern,
        grid=(bsz, seq // ts),
        in_specs=[
            pl.BlockSpec((1, ts, 2 * D_RNN), lambda b, s: (b, s, 0)),
            _const_spec((CONV_W, D_RNN)),
            _const_spec((1, D_RNN)),
            _const_spec((RNN_BLOCKS, RNN_BW, 2 * RNN_BW)),
            _const_spec((RNN_BLOCKS, 1, 2 * RNN_BW)),
            _const_spec((1, D_RNN)),
        ],
        out_specs=pl.BlockSpec((1, ts, D_RNN), lambda b, s: (b, s, 0)),
        out_shape=jax.ShapeDtypeStruct((bsz, seq, D_RNN), BF16),
        scratch_shapes=[
            pltpu.VMEM((ts + SUBLANES, D_RNN), F32),
            pltpu.VMEM((SUBLANES, D_RNN), F32),
        ],
        compiler_params=pltpu.CompilerParams(
            dimension_semantics=("arbitrary", "arbitrary"), vmem_limit_bytes=VMEM_LIMIT),
        name="rglru",
    )(rnn3, cw, cb, wri, bri, lam)


def _moba_kernel(q_ref, k_ref, v_ref, km_ref, o_ref, km_sc, m_sc, l_sc, acc_sc, *, nblk):
    j = pl.program_id(2)
    bq = MOBA_BLOCK

    @pl.when(j == 0)
    def _():
        km_sc[...] = jnp.zeros_like(km_sc)
        km_sc[0:nblk, :] = km_ref[0, :, 0, :]

    q = q_ref[0, 0]

    gate = lax.dot_general(q.astype(F32), km_sc[...], (((1,), (1,)), ((), ())),
                           precision=lax.Precision.HIGHEST, preferred_element_type=F32)
    lane = lax.broadcasted_iota(jnp.int32, (bq, LANES), 1)
    lane_f = lane.astype(F32)
    g = jnp.where(lane < j, gate, NEG)
    sel = jnp.zeros((bq, LANES), dtype=jnp.bool_)
    for slot in range(MOBA_TOPK):
        gmax = jnp.max(g, axis=-1, keepdims=True)
        idx = jnp.min(jnp.where(g == gmax, lane_f, float(LANES)), axis=-1, keepdims=True)
        hit = lane_f == idx
        sel = jnp.logical_or(sel, jnp.logical_and(hit, slot < j))
        g = jnp.where(hit, -3e38, g)
    negsel = jnp.where(sel, 0.0, NEG).astype(BF16)
    q_aug = jnp.concatenate([q, negsel], axis=1)

    own = pl.multiple_of(j * bq, bq)
    k_own = k_ref[0, 0, pl.ds(own, bq), 0:HEAD_DIM]
    s = lax.dot_general(q, k_own, (((1,), (1,)), ((), ())), preferred_element_type=F32)
    r_i = lax.broadcasted_iota(jnp.int32, (bq, bq), 0)
    c_i = lax.broadcasted_iota(jnp.int32, (bq, bq), 1)
    s = jnp.where(c_i <= r_i, s, NEG)
    m0 = jnp.max(s, axis=-1, keepdims=True)
    p = jnp.exp(s - m0)
    m_sc[...] = m0
    l_sc[...] = jnp.sum(p, axis=-1, keepdims=True)
    acc_sc[...] = jnp.dot(p.astype(BF16), v_ref[0, 0, pl.ds(own, bq), :], preferred_element_type=F32)

    def body(i, carry):
        st = pl.multiple_of(i * bq, bq)
        kb = k_ref[0, 0, pl.ds(st, bq), :]
        vb = v_ref[0, 0, pl.ds(st, bq), :]
        sc = lax.dot_general(q_aug, kb, (((1,), (1,)), ((), ())), preferred_element_type=F32)
        m_prev = m_sc[...]
        m_new = jnp.maximum(m_prev, jnp.max(sc, axis=-1, keepdims=True))
        alpha = jnp.exp(m_prev - m_new)
        pb = jnp.exp(sc - m_new)
        l_sc[...] = alpha * l_sc[...] + jnp.sum(pb, axis=-1, keepdims=True)
        acc_sc[...] = alpha * acc_sc[...] + jnp.dot(pb.astype(BF16), vb, preferred_element_type=F32)
        m_sc[...] = m_new
        return carry

    lax.fori_loop(0, j, body, 0)
    o_ref[0] = (acc_sc[...] / l_sc[...]).astype(BF16)


def _moba(q, k, v, km):
    bsz, nh, seq, _ = q.shape
    nblk = seq // MOBA_BLOCK
    kern = functools.partial(_moba_kernel, nblk=nblk)
    return pl.pallas_call(
        kern,
        grid=(bsz, nh, nblk),
        in_specs=[
            pl.BlockSpec((1, 1, MOBA_BLOCK, HEAD_DIM), lambda b, h, j: (b, h, j, 0)),
            pl.BlockSpec((1, 1, seq, 2 * HEAD_DIM), lambda b, h, j: (b, h, 0, 0)),
            pl.BlockSpec((1, 1, seq, HEAD_DIM), lambda b, h, j: (b, h, 0, 0)),
            pl.BlockSpec((1, nblk, 1, HEAD_DIM), lambda b, h, j: (b, 0, 0, h)),
        ],
        out_specs=pl.BlockSpec((1, MOBA_BLOCK, HEAD_DIM), lambda b, h, j: (b, j, h)),
        out_shape=jax.ShapeDtypeStruct((bsz, seq, D_ATTN), BF16),
        scratch_shapes=[
            pltpu.VMEM((LANES, HEAD_DIM), F32),
            pltpu.VMEM((MOBA_BLOCK, 1), F32),
            pltpu.VMEM((MOBA_BLOCK, 1), F32),
            pltpu.VMEM((MOBA_BLOCK, HEAD_DIM), F32),
        ],
        compiler_params=pltpu.CompilerParams(
            dimension_semantics=("arbitrary", "arbitrary", "arbitrary"), vmem_limit_bytes=VMEM_LIMIT),
        name="moba",
    )(q, k, v, km)


def _merge_kernel(ya_ref, yb_ref, gate_ref, x_ref, pa_ref, pb_ref, wo_ref, o_ref):
    a = jnp.dot(ya_ref[...], pa_ref[...], preferred_element_type=F32)
    b = jnp.dot(yb_ref[...], pb_ref[...], preferred_element_type=F32)
    merged = gate_ref[:, 0:D_MODEL] * a + gate_ref[:, D_MODEL:2 * D_MODEL] * b
    o_ref[...] = x_ref[...] + jnp.dot(merged.astype(BF16), wo_ref[...], preferred_element_type=F32)


def _merge(ya, yb, gates, x2, pa, pb, wo, *, tm):
    t = x2.shape[0]
    row = lambda w: pl.BlockSpec((tm, w), lambda i: (i, 0))
    return pl.pallas_call(
        _merge_kernel,
        grid=(t // tm,),
        in_specs=[row(D_RNN), row(D_ATTN), row(2 * D_MODEL), row(D_MODEL),
                  _const_spec((D_RNN, D_MODEL)), _const_spec((D_ATTN, D_MODEL)),
                  _const_spec((D_MODEL, D_MODEL))],
        out_specs=row(D_MODEL),
        out_shape=jax.ShapeDtypeStruct((t, D_MODEL), F32),
        compiler_params=pltpu.CompilerParams(
            dimension_semantics=("arbitrary",), vmem_limit_bytes=VMEM_LIMIT),
        name="merge",
    )(ya, yb, gates, x2, pa, pb, wo)


def _ffn_kernel(x_ref, g2_ref, wu_ref, wg_ref, cw_ref, cb_ref, wd_ref, o_ref, ubuf, *, tm):
    s = pl.program_id(1)

    @pl.when(s == 0)
    def _():
        ubuf[...] = jnp.zeros_like(ubuf)

    x = x_ref[0]
    hb = _rms(x, g2_ref[...]).astype(BF16)
    ubuf[0:SUBLANES, :] = ubuf[tm:tm + SUBLANES, :]
    ubuf[SUBLANES:SUBLANES + tm, :] = jnp.dot(hb, wu_ref[...], preferred_element_type=F32)
    conv = cb_ref[...]
    for k in range(FFN_CONV_W):
        off = SUBLANES - (FFN_CONV_W - 1) + k
        conv = conv + ubuf[off:off + tm, :] * cw_ref[k:k + 1, :]
    act = _gelu(conv) * jnp.dot(hb, wg_ref[...], preferred_element_type=F32)
    o_ref[0] = x + jnp.dot(act.astype(BF16), wd_ref[...], preferred_element_type=F32)


def _ffn(x3, g2, wu, wg, cw, cb, wd, *, tm):
    bsz, seq, _ = x3.shape
    kern = functools.partial(_ffn_kernel, tm=tm)
    return pl.pallas_call(
        kern,
        grid=(bsz, seq // tm),
        in_specs=[
            pl.BlockSpec((1, tm, D_MODEL), lambda b, s: (b, s, 0)),
            _const_spec((1, D_MODEL)),
            _const_spec((D_MODEL, D_FF)),
            _const_spec((D_MODEL, D_FF)),
            _const_spec((FFN_CONV_W, D_FF)),
            _const_spec((1, D_FF)),
            _const_spec((D_FF, D_MODEL)),
        ],
        out_specs=pl.BlockSpec((1, tm, D_MODEL), lambda b, s: (b, s, 0)),
        out_shape=jax.ShapeDtypeStruct((bsz, seq, D_MODEL), F32),
        scratch_shapes=[pltpu.VMEM((tm + SUBLANES, D_FF), F32)],
        compiler_params=pltpu.CompilerParams(
            dimension_semantics=("arbitrary", "arbitrary"), vmem_limit_bytes=VMEM_LIMIT),
        name="ffn",
    )(x3, g2, wu, wg, cw, cb, wd)


def _tile(seq, want):
    t = min(want, seq)
    assert seq % t == 0 and t % MOBA_BLOCK == 0
    return t


def kernel(x, norm1_g, w_in, conv_w, conv_b, w_r, b_r, w_i, b_i, lru_lambda, q_norm_g, k_norm_g,
           w_proj_rnn, w_proj_attn, w_out, norm2_g, w_up, w_gate, ffn_conv_w, ffn_conv_b, w_down):
    bsz, seq, d = x.shape
    assert d == D_MODEL and seq % MOBA_BLOCK == 0 and seq // MOBA_BLOCK <= LANES
    depth = norm1_g.shape[0]
    t = bsz * seq
    for l in range(depth):
        x2 = x.reshape(t, D_MODEL)
        rnn, q, k, v, km, gates = _in_proj(
            x2, norm1_g[l][None], w_in[l].astype(BF16), q_norm_g[l][None], k_norm_g[l][None],
            bsz=bsz, seq=seq, tm=_tile(seq, 256))
        wri = jnp.concatenate([w_r[l], w_i[l]], axis=-1).astype(BF16)
        bri = jnp.concatenate([b_r[l], b_i[l]], axis=-1)[:, None, :]
        ya = _rglru(rnn.reshape(bsz, seq, 2 * D_RNN), conv_w[l], conv_b[l][None], wri, bri,
                    lru_lambda[l][None], ts=_tile(seq, 256) // 2)
        yb = _moba(q, k, v, km)
        x1 = _merge(ya.reshape(t, D_RNN), yb.reshape(t, D_ATTN), gates, x2,
                    w_proj_rnn[l].astype(BF16), w_proj_attn[l].astype(BF16), w_out[l].astype(BF16),
                    tm=_tile(seq, 512))
        x = _ffn(x1.reshape(bsz, seq, D_MODEL), norm2_g[l][None], w_up[l].astype(BF16),
                 w_gate[l].astype(BF16), ffn_conv_w[l], ffn_conv_b[l][None], w_down[l].astype(BF16),
                 tm=_tile(seq, 256))
    return x
```

```python
import functools

import jax
import jax.numpy as jnp
from jax import lax
from jax.experimental import pallas as pl
from jax.experimental.pallas import tpu as pltpu

D_MODEL = 1024
D_RNN = 1024
RNN_BLOCKS = 8
RNN_BW = D_RNN // RNN_BLOCKS
CONV_W = 4
LRU_C = 8.0
N_HEADS = 8
HEAD_DIM = 128
D_ATTN = N_HEADS * HEAD_DIM
MOBA_BLOCK = 256
MOBA_TOPK = 3
D_FF = 2816
FFN_CONV_W = 3
EPS = 1e-6
NEG = -1e30
LOG2_E = 1.4426950408889634

LANES = 128
SUBLANES = 8
VMEM_LIMIT = 56 * 1024 * 1024

F32 = jnp.float32
BF16 = jnp.bfloat16

C_RNN = 0
C_Q = 2 * D_RNN
C_K = C_Q + D_ATTN
C_V = C_K + D_ATTN
C_G = C_V + D_ATTN
D_IN = C_G + 2 * D_MODEL


def _rms(y, g):
    return y * lax.rsqrt(jnp.mean(y * y, axis=-1, keepdims=True) + EPS) * g


def _gelu(x):
    c = 0.7978845608028654
    return 0.5 * x * (1.0 + jnp.tanh(c * (x + 0.044715 * (x * x * x))))


def _const_spec(shape):
    nd = len(shape)
    return pl.BlockSpec(shape, lambda *_: (0,) * nd, pipeline_mode=pl.Buffered(1))


def _in_proj_kernel(x_ref, g1_ref, w_ref, qg_ref, kg_ref,
                    rnn_ref, q_ref, k_ref, v_ref, gate_ref, km_sc, *, tiles_per_seq):
    tm = MOBA_BLOCK
    j = pl.program_id(0) % tiles_per_seq
    nt_dims = (((1,), (1,)), ((), ()))

    @pl.when(j == 0)
    def _():
        km_sc[...] = jnp.zeros_like(km_sc)

    hb = _rms(x_ref[...], g1_ref[...]).astype(BF16)

    def proj(c0, c1):
        return jnp.dot(hb, w_ref[:, c0:c1], preferred_element_type=F32)

    rnn_ref[...] = proj(C_RNN, C_Q)

    nrow = km_sc.shape[0]
    blk = lax.broadcasted_iota(jnp.int32, (nrow, tm), 0)
    blk_f = blk.astype(F32)
    scale = HEAD_DIM ** -0.5 * LOG2_E
    yq = proj(C_Q, C_K)
    for h in range(N_HEADS):
        cols = slice(h * HEAD_DIM, (h + 1) * HEAD_DIM)
        y = _rms(yq[:, cols], qg_ref[...]) * scale
        q_ref[0, h, :, 0:HEAD_DIM] = y.astype(BF16)
        gate_t = lax.dot_general(km_sc[:, cols], y, nt_dims,
                                 precision=lax.Precision.HIGHEST, preferred_element_type=F32)
        g = jnp.where(blk < j, gate_t, NEG)
        sel = jnp.zeros((nrow, tm), dtype=jnp.bool_)
        for slot in range(MOBA_TOPK):
            gmax = jnp.max(g, axis=0, keepdims=True)
            idx = jnp.min(jnp.where(g == gmax, blk_f, float(LANES)), axis=0, keepdims=True)
            hit = blk_f == idx
            sel = jnp.logical_or(sel, jnp.logical_and(hit, slot < j))
            g = jnp.where(hit, -3e38, g)
        negsel_t = jnp.where(sel, 0.0, NEG)
        if nrow < LANES:
            negsel_t = jnp.concatenate([negsel_t, jnp.zeros((LANES - nrow, tm), F32)], axis=0)
        q_ref[0, h, :, HEAD_DIM:2 * HEAD_DIM] = jnp.transpose(negsel_t).astype(BF16)

    lane = lax.broadcasted_iota(jnp.int32, (tm, LANES), 1)
    onehot = jnp.where(lane == j, 1.0, 0.0).astype(BF16)
    yk = proj(C_K, C_V)
    for h in range(N_HEADS):
        cols = slice(h * HEAD_DIM, (h + 1) * HEAD_DIM)
        y = _rms(yk[:, cols], kg_ref[...])
        k_ref[0, h, :, 0:HEAD_DIM] = y.astype(BF16)
        k_ref[0, h, :, HEAD_DIM:2 * HEAD_DIM] = onehot
        km_sc[:, cols] = jnp.where(blk[:, 0:HEAD_DIM] == j, jnp.mean(y, axis=0, keepdims=True), km_sc[:, cols])

    yv = proj(C_V, C_G)
    ones = jnp.ones((tm, HEAD_DIM), BF16)
    for h in range(N_HEADS):
        v_ref[0, h, :, 0:HEAD_DIM] = yv[:, h * HEAD_DIM:(h + 1) * HEAD_DIM].astype(BF16)
        v_ref[0, h, :, HEAD_DIM:2 * HEAD_DIM] = ones

    gate_ref[...] = jax.nn.sigmoid(proj(C_G, D_IN))


def _in_proj(x2, g1, w_in, qg, kg, *, bsz, seq):
    t = bsz * seq
    tm = MOBA_BLOCK
    nst = seq // tm
    nrow = -(-nst // SUBLANES) * SUBLANES
    kern = functools.partial(_in_proj_kernel, tiles_per_seq=nst)
    return pl.pallas_call(
        kern,
        grid=(t // tm,),
        in_specs=[
            pl.BlockSpec((tm, D_MODEL), lambda i: (i, 0)),
            _const_spec((1, D_MODEL)),
            _const_spec((D_MODEL, D_IN)),
            _const_spec((1, HEAD_DIM)),
            _const_spec((1, HEAD_DIM)),
        ],
        out_specs=[
            pl.BlockSpec((tm, 2 * D_RNN), lambda i: (i, 0)),
            pl.BlockSpec((1, N_HEADS, tm, 2 * HEAD_DIM), lambda i: (i // nst, 0, i % nst, 0)),
            pl.BlockSpec((1, N_HEADS, tm, 2 * HEAD_DIM), lambda i: (i // nst, 0, i % nst, 0)),
            pl.BlockSpec((1, N_HEADS, tm, 2 * HEAD_DIM), lambda i: (i // nst, 0, i % nst, 0)),
            pl.BlockSpec((tm, 2 * D_MODEL), lambda i: (i, 0)),
        ],
        out_shape=[
            jax.ShapeDtypeStruct((t, 2 * D_RNN), F32),
            jax.ShapeDtypeStruct((bsz, N_HEADS, seq, 2 * HEAD_DIM), BF16),
            jax.ShapeDtypeStruct((bsz, N_HEADS, seq, 2 * HEAD_DIM), BF16),
            jax.ShapeDtypeStruct((bsz, N_HEADS, seq, 2 * HEAD_DIM), BF16),
            jax.ShapeDtypeStruct((t, 2 * D_MODEL), F32),
        ],
        scratch_shapes=[pltpu.VMEM((nrow, D_ATTN), F32)],
        compiler_params=pltpu.CompilerParams(
            dimension_semantics=("arbitrary",), vmem_limit_bytes=VMEM_LIMIT),
        name="in_proj",
    )(x2, g1, w_in, qg, kg)


def _shift_rows(x, d, fill, row):
    return jnp.where(row >= d, pltpu.roll(x, d, 0), fill)


def _rglru_kernel(rnn_ref, cw_ref, cb_ref, wri_ref, bri_ref, lam_ref, ya_ref, xbuf, hst, *, ts):
    s = pl.program_id(1)

    @pl.when(s == 0)
    def _():
        xbuf[...] = jnp.zeros_like(xbuf)
        hst[...] = jnp.zeros_like(hst)

    xbuf[0:SUBLANES, :] = xbuf[ts:ts + SUBLANES, :]
    xbuf[SUBLANES:SUBLANES + ts, :] = rnn_ref[0, :, 0:D_RNN]

    row = lax.broadcasted_iota(jnp.int32, (ts, RNN_BW), 0)
    for n in range(RNN_BLOCKS):
        cols = slice(n * RNN_BW, (n + 1) * RNN_BW)
        xa = cb_ref[:, cols]
        for k in range(CONV_W):
            off = SUBLANES - (CONV_W - 1) + k
            xa = xa + xbuf[off:off + ts, cols] * cw_ref[k:k + 1, cols]
        z = jnp.dot(xa.astype(BF16), wri_ref[n], preferred_element_type=F32) + bri_ref[n]
        r = jax.nn.sigmoid(z[:, 0:RNN_BW])
        gi = jax.nn.sigmoid(z[:, RNN_BW:2 * RNN_BW])
        log_a = (-LRU_C * r) * jax.nn.softplus(-lam_ref[:, cols])
        a = jnp.exp(log_a)
        b = jnp.sqrt(-jnp.tanh(log_a) * (a * a + 1.0)) * (gi * xa)
        d = 1
        while d < ts:
            a_sh = _shift_rows(a, d, 1.0, row)
            b_sh = _shift_rows(b, d, 0.0, row)
            b = a * b_sh + b
            a = a * a_sh
            d *= 2
        h = b + a * hst[0:1, cols]
        hst[0:1, cols] = h[ts - 1:ts, :]
        ya_ref[0, :, cols] = (h * _gelu(rnn_ref[0, :, D_RNN + n * RNN_BW:D_RNN + (n + 1) * RNN_BW])).astype(BF16)


def _rglru(rnn3, cw, cb, wri, bri, lam, *, ts):
    bsz, seq, _ = rnn3.shape
    kern = functools.partial(_rglru_kernel, ts=ts)
    return pl.pallas_call(
        kern,
        grid=(bsz, seq // ts),
        in_specs=[
            pl.BlockSpec((1, ts, 2 * D_RNN), lambda b, s: (b, s, 0)),
            _const_spec((CONV_W, D_RNN)),
            _const_spec((1, D_RNN)),
            _const_spec((RNN_BLOCKS, RNN_BW, 2 * RNN_BW)),
            _const_spec((RNN_BLOCKS, 1, 2 * RNN_BW)),
            _const_spec((1, D_RNN)),
        ],
        out_specs=pl.BlockSpec((1, ts, D_RNN), lambda b, s: (b, s, 0)),
        out_shape=jax.ShapeDtypeStruct((bsz, seq, D_RNN), BF16),
        scratch_shapes=[
            pltpu.VMEM((ts + SUBLANES, D_RNN), F32),
            pltpu.VMEM((SUBLANES, D_RNN), F32),
        ],
        compiler_params=pltpu.CompilerParams(
            dimension_semantics=("arbitrary", "arbitrary"), vmem_limit_bytes=VMEM_LIMIT),
        name="rglru",
    )(rnn3, cw, cb, wri, bri, lam)


def _moba_kernel(q_ref, k_ref, v_ref, o_ref, sa_sc, sb_sc, m_sc, acc_sc, *, nchunk, cblk, hp):
    j = pl.program_id(2)
    bq = MOBA_BLOCK
    kc = cblk * MOBA_BLOCK
    nt_dims = (((1,), (1,)), ((), ()))
    heads = range(hp)
    last = nchunk - 1

    def scores(h, chunk, dst):
        st = pl.multiple_of(chunk * kc, kc)
        dst[h] = lax.dot_general(q_ref[0, h], k_ref[0, h, pl.ds(st, kc), :], nt_dims,
                                 preferred_element_type=F32)

    def accumulate(h, chunk, src):
        st = pl.multiple_of(chunk * kc, kc)
        sc = src[h]
        m_prev = m_sc[h]
        m_new = jnp.maximum(m_prev, jnp.max(sc, axis=-1, keepdims=True))
        p = jnp.exp2(sc - m_new).astype(BF16)
        acc_sc[h] = jnp.exp2(m_prev - m_new) * acc_sc[h] + jnp.dot(
            p, v_ref[0, h, pl.ds(st, kc), :], preferred_element_type=F32)
        m_sc[h] = m_new

    r_i = lax.broadcasted_iota(jnp.int32, (bq, bq), 0)
    c_i = lax.broadcasted_iota(jnp.int32, (bq, bq), 1)
    own = pl.multiple_of(j * bq, bq)
    for h in heads:
        s = lax.dot_general(q_ref[0, h, :, 0:HEAD_DIM], k_ref[0, h, pl.ds(own, bq), 0:HEAD_DIM],
                            nt_dims, preferred_element_type=F32)
        s = jnp.where(c_i <= r_i, s, NEG)
        m0 = jnp.max(s, axis=-1, keepdims=True)
        m_sc[h] = m0
        acc_sc[h] = jnp.dot(jnp.exp2(s - m0).astype(BF16), v_ref[0, h, pl.ds(own, bq), :],
                            preferred_element_type=F32)
        scores(h, 0, sa_sc)

    def body(t, carry):
        c0 = 2 * t
        for h in heads:
            scores(h, c0 + 1, sb_sc)
        for h in heads:
            accumulate(h, c0, sa_sc)
        for h in heads:
            scores(h, jnp.minimum(c0 + 2, last), sa_sc)
        for h in heads:
            accumulate(h, c0 + 1, sb_sc)
        return carry

    n_past = lax.div(j + (cblk - 1), cblk)
    lax.fori_loop(0, lax.div(n_past + 1, 2), body, 0)
    for h in heads:
        acc = acc_sc[h]
        o_ref[0, :, h * HEAD_DIM:(h + 1) * HEAD_DIM] = (
            acc[:, 0:HEAD_DIM] / acc[:, HEAD_DIM:2 * HEAD_DIM]).astype(BF16)


def _moba(q, k, v, *, hp, cblk):
    bsz, nh, seq, _ = q.shape
    nblk = seq // MOBA_BLOCK
    assert nblk % (2 * cblk) == 0 and nh % hp == 0
    kc = cblk * MOBA_BLOCK
    kern = functools.partial(_moba_kernel, nchunk=nblk // cblk, cblk=cblk, hp=hp)
    return pl.pallas_call(
        kern,
        grid=(bsz, nh // hp, nblk),
        in_specs=[
            pl.BlockSpec((1, hp, MOBA_BLOCK, 2 * HEAD_DIM), lambda b, h, j: (b, h, j, 0)),
            pl.BlockSpec((1, hp, seq, 2 * HEAD_DIM), lambda b, h, j: (b, h, 0, 0)),
            pl.BlockSpec((1, hp, seq, 2 * HEAD_DIM), lambda b, h, j: (b, h, 0, 0)),
        ],
        out_specs=pl.BlockSpec((1, MOBA_BLOCK, hp * HEAD_DIM), lambda b, h, j: (b, j, h)),
        out_shape=jax.ShapeDtypeStruct((bsz, seq, D_ATTN), BF16),
        scratch_shapes=[
            pltpu.VMEM((hp, MOBA_BLOCK, kc), F32),
            pltpu.VMEM((hp, MOBA_BLOCK, kc), F32),
            pltpu.VMEM((hp, MOBA_BLOCK, 1), F32),
            pltpu.VMEM((hp, MOBA_BLOCK, 2 * HEAD_DIM), F32),
        ],
        compiler_params=pltpu.CompilerParams(
            dimension_semantics=("arbitrary", "arbitrary", "arbitrary"), vmem_limit_bytes=VMEM_LIMIT),
        name="moba",
    )(q, k, v)


def _merge_kernel(ya_ref, yb_ref, gate_ref, x_ref, pa_ref, pb_ref, wo_ref, o_ref):
    a = jnp.dot(ya_ref[...], pa_ref[...], preferred_element_type=F32)
    b = jnp.dot(yb_ref[...], pb_ref[...], preferred_element_type=F32)
    merged = gate_ref[:, 0:D_MODEL] * a + gate_ref[:, D_MODEL:2 * D_MODEL] * b
    o_ref[...] = x_ref[...] + jnp.dot(merged.astype(BF16), wo_ref[...], preferred_element_type=F32)


def _merge(ya, yb, gates, x2, pa, pb, wo, *, tm):
    t = x2.shape[0]
    row = lambda w: pl.BlockSpec((tm, w), lambda i: (i, 0))
    return pl.pallas_call(
        _merge_kernel,
        grid=(t // tm,),
        in_specs=[row(D_RNN), row(D_ATTN), row(2 * D_MODEL), row(D_MODEL),
                  _const_spec((D_RNN, D_MODEL)), _const_spec((D_ATTN, D_MODEL)),
                  _const_spec((D_MODEL, D_MODEL))],
        out_specs=row(D_MODEL),
        out_shape=jax.ShapeDtypeStruct((t, D_MODEL), F32),
        compiler_params=pltpu.CompilerParams(
            dimension_semantics=("arbitrary",), vmem_limit_bytes=VMEM_LIMIT),
        name="merge",
    )(ya, yb, gates, x2, pa, pb, wo)


def _ffn_kernel(x_ref, g2_ref, wu_ref, wg_ref, cw_ref, cb_ref, wd_ref, o_ref, ubuf, *, tm):
    s = pl.program_id(1)

    @pl.when(s == 0)
    def _():
        ubuf[...] = jnp.zeros_like(ubuf)

    x = x_ref[0]
    hb = _rms(x, g2_ref[...]).astype(BF16)
    ubuf[0:SUBLANES, :] = ubuf[tm:tm + SUBLANES, :]
    ubuf[SUBLANES:SUBLANES + tm, :] = jnp.dot(hb, wu_ref[...], preferred_element_type=F32)
    conv = cb_ref[...]
    for k in range(FFN_CONV_W):
        off = SUBLANES - (FFN_CONV_W - 1) + k
        conv = conv + ubuf[off:off + tm, :] * cw_ref[k:k + 1, :]
    act = _gelu(conv) * jnp.dot(hb, wg_ref[...], preferred_element_type=F32)
    o_ref[0] = x + jnp.dot(act.astype(BF16), wd_ref[...], preferred_element_type=F32)


def _ffn(x3, g2, wu, wg, cw, cb, wd, *, tm):
    bsz, seq, _ = x3.shape
    kern = functools.partial(_ffn_kernel, tm=tm)
    return pl.pallas_call(
        kern,
        grid=(bsz, seq // tm),
        in_specs=[
            pl.BlockSpec((1, tm, D_MODEL), lambda b, s: (b, s, 0)),
            _const_spec((1, D_MODEL)),
            _const_spec((D_MODEL, D_FF)),
            _const_spec((D_MODEL, D_FF)),
            _const_spec((FFN_CONV_W, D_FF)),
            _const_spec((1, D_FF)),
            _const_spec((D_FF, D_MODEL)),
        ],
        out_specs=pl.BlockSpec((1, tm, D_MODEL), lambda b, s: (b, s, 0)),
        out_shape=jax.ShapeDtypeStruct((bsz, seq, D_MODEL), F32),
        scratch_shapes=[pltpu.VMEM((tm + SUBLANES, D_FF), F32)],
        compiler_params=pltpu.CompilerParams(
            dimension_semantics=("arbitrary", "arbitrary"), vmem_limit_bytes=VMEM_LIMIT),
        name="ffn",
    )(x3, g2, wu, wg, cw, cb, wd)


def _tile(seq, want):
    t = min(want, seq)
    assert seq % t == 0 and t % MOBA_BLOCK == 0
    return t


def kernel(x, norm1_g, w_in, conv_w, conv_b, w_r, b_r, w_i, b_i, lru_lambda, q_norm_g, k_norm_g,
           w_proj_rnn, w_proj_attn, w_out, norm2_g, w_up, w_gate, ffn_conv_w, ffn_conv_b, w_down):
    bsz, seq, d = x.shape
    assert d == D_MODEL and seq % MOBA_BLOCK == 0 and seq // MOBA_BLOCK <= LANES
    depth = norm1_g.shape[0]
    t = bsz * seq
    for l in range(depth):
        x2 = x.reshape(t, D_MODEL)
        rnn, q, k, v, gates = _in_proj(
            x2, norm1_g[l][None], w_in[l].astype(BF16), q_norm_g[l][None], k_norm_g[l][None],
            bsz=bsz, seq=seq)
        wri = jnp.concatenate([w_r[l], w_i[l]], axis=-1).astype(BF16)
        bri = jnp.concatenate([b_r[l], b_i[l]], axis=-1)[:, None, :]
        ya = _rglru(rnn.reshape(bsz, seq, 2 * D_RNN), conv_w[l], conv_b[l][None], wri, bri,
                    lru_lambda[l][None], ts=_tile(seq, 256) // 2)
        yb = _moba(q, k, v, hp=2, cblk=2)
        x1 = _merge(ya.reshape(t, D_RNN), yb.reshape(t, D_ATTN), gates, x2,
                    w_proj_rnn[l].astype(BF16), w_proj_attn[l].astype(BF16), w_out[l].astype(BF16),
                    tm=_tile(seq, 512))
        x = _ffn(x1.reshape(bsz, seq, D_MODEL), norm2_g[l][None], w_up[l].astype(BF16),
                 w_gate[l].astype(BF16), ffn_conv_w[l], ffn_conv_b[l][None], w_down[l].astype(BF16),
                 tm=_tile(seq, 256))
    return x
```

```python
import functools

import jax
import jax.numpy as jnp
from jax import lax
from jax.experimental import pallas as pl
from jax.experimental.pallas import tpu as pltpu

D_MODEL = 1024
D_RNN = 1024
RNN_BLOCKS = 8
RNN_BW = D_RNN // RNN_BLOCKS
CONV_W = 4
LRU_C = 8.0
N_HEADS = 8
HEAD_DIM = 128
D_ATTN = N_HEADS * HEAD_DIM
MOBA_BLOCK = 256
MOBA_TOPK = 3
D_FF = 2816
FFN_CONV_W = 3
EPS = 1e-6
NEG = -1e30
LOG2_E = 1.4426950408889634

LANES = 128
SUBLANES = 8
V_ROWS = HEAD_DIM + 2 * SUBLANES
VMEM_LIMIT = 56 * 1024 * 1024

F32 = jnp.float32
BF16 = jnp.bfloat16

C_RNN = 0
C_Q = 2 * D_RNN
C_K = C_Q + D_ATTN
C_V = C_K + D_ATTN
C_G = C_V + D_ATTN
D_IN = C_G + 2 * D_MODEL


def _rms(y, g):
    return y * lax.rsqrt(jnp.mean(y * y, axis=-1, keepdims=True) + EPS) * g


def _gelu(x):
    c = 0.7978845608028654
    return 0.5 * x * (1.0 + jnp.tanh(c * (x + 0.044715 * (x * x * x))))


def _const_spec(shape):
    nd = len(shape)
    return pl.BlockSpec(shape, lambda *_: (0,) * nd, pipeline_mode=pl.Buffered(1))


def _lru_scan(a, b, h0):
    rows, width = a.shape
    groups = rows // SUBLANES
    a = a.reshape(groups, SUBLANES, width)
    b = b.reshape(groups, SUBLANES, width)
    sub = lax.broadcasted_iota(jnp.int32, a.shape, 1)
    for d in (1, 2, 4):
        a_sh = jnp.where(sub >= d, pltpu.roll(a, d, 1), 1.0)
        b_sh = jnp.where(sub >= d, pltpu.roll(b, d, 1), 0.0)
        b = a * b_sh + b
        a = a * a_sh
    out = []
    for g in range(groups):
        hg = b[g] + a[g] * h0
        h0 = hg[SUBLANES - 1:SUBLANES]
        out.append(hg)
    return jnp.concatenate(out, axis=0), h0


def _in_proj_kernel(x_ref, g1_ref, w_ref, qg_ref, kg_ref, cw_ref, cb_ref, wri_ref, bri_ref, lam_ref,
                    ya_ref, q_ref, k_ref, v_ref, gate_ref, km_sc, xbuf, hst, *, tiles_per_seq):
    tm = MOBA_BLOCK
    j = pl.program_id(0) % tiles_per_seq
    nt_dims = (((1,), (1,)), ((), ()))

    @pl.when(j == 0)
    def _():
        km_sc[...] = jnp.zeros_like(km_sc)
        xbuf[...] = jnp.zeros_like(xbuf)
        hst[...] = jnp.zeros_like(hst)

    hb = _rms(x_ref[...], g1_ref[...]).astype(BF16)

    def proj(c0, c1):
        return jnp.dot(hb, w_ref[:, c0:c1], preferred_element_type=F32)

    rnn = proj(C_RNN, C_Q)
    xbuf[0:SUBLANES, :] = xbuf[tm:tm + SUBLANES, :]
    xbuf[SUBLANES:SUBLANES + tm, :] = rnn[:, 0:D_RNN]
    for n in range(RNN_BLOCKS):
        cols = slice(n * RNN_BW, (n + 1) * RNN_BW)
        xa = cb_ref[:, cols]
        for k in range(CONV_W):
            off = SUBLANES - (CONV_W - 1) + k
            xa = xa + xbuf[off:off + tm, cols] * cw_ref[k:k + 1, cols]
        z = jnp.dot(xa.astype(BF16), wri_ref[n], preferred_element_type=F32) + bri_ref[n]
        r = jax.nn.sigmoid(z[:, 0:RNN_BW])
        gi = jax.nn.sigmoid(z[:, RNN_BW:2 * RNN_BW])
        log_a = (-LRU_C * r) * jax.nn.softplus(-lam_ref[:, cols])
        a = jnp.exp(log_a)
        w = -jnp.tanh(log_a) * (a * a + 1.0)
        b = jnp.where(w > 0.0, w * lax.rsqrt(w), 0.0) * (gi * xa)
        hseq, hlast = _lru_scan(a, b, hst[0:1, cols])
        hst[0:1, cols] = hlast
        ya_ref[:, cols] = (hseq * _gelu(rnn[:, D_RNN + n * RNN_BW:D_RNN + (n + 1) * RNN_BW])).astype(BF16)

    nrow = km_sc.shape[0]
    blk = lax.broadcasted_iota(jnp.int32, (nrow, tm), 0)
    blk_f = blk.astype(F32)
    scale = HEAD_DIM ** -0.5 * LOG2_E
    yq = proj(C_Q, C_K)
    for h in range(N_HEADS):
        cols = slice(h * HEAD_DIM, (h + 1) * HEAD_DIM)
        y = _rms(yq[:, cols], qg_ref[...]) * scale
        q_ref[0, h, :, 0:HEAD_DIM] = y.astype(BF16)
        gate_t = lax.dot_general(km_sc[:, cols], y, nt_dims,
                                 precision=lax.Precision.HIGHEST, preferred_element_type=F32)
        g = jnp.where(blk < j, gate_t, NEG)
        sel = jnp.zeros((nrow, tm), dtype=jnp.bool_)
        for slot in range(MOBA_TOPK):
            gmax = jnp.max(g, axis=0, keepdims=True)
            idx = jnp.min(jnp.where(g == gmax, blk_f, float(LANES)), axis=0, keepdims=True)
            hit = blk_f == idx
            sel = jnp.logical_or(sel, jnp.logical_and(hit, slot < j))
            g = jnp.where(hit, -3e38, g)
        negsel_t = jnp.where(sel, 0.0, NEG)
        if nrow < LANES:
            negsel_t = jnp.concatenate([negsel_t, jnp.zeros((LANES - nrow, tm), F32)], axis=0)
        q_ref[0, h, :, HEAD_DIM:2 * HEAD_DIM] = jnp.transpose(negsel_t).astype(BF16)

    lane = lax.broadcasted_iota(jnp.int32, (tm, LANES), 1)
    onehot = jnp.where(lane == j, 1.0, 0.0).astype(BF16)
    yk = proj(C_K, C_V)
    for h in range(N_HEADS):
        cols = slice(h * HEAD_DIM, (h + 1) * HEAD_DIM)
        y = _rms(yk[:, cols], kg_ref[...])
        k_ref[0, h, :, 0:HEAD_DIM] = y.astype(BF16)
        k_ref[0, h, :, HEAD_DIM:2 * HEAD_DIM] = onehot
        km_sc[:, cols] = jnp.where(blk[:, 0:HEAD_DIM] == j, jnp.mean(y, axis=0, keepdims=True), km_sc[:, cols])

    yv = proj(C_V, C_G)
    ones = jnp.ones((V_ROWS - HEAD_DIM, tm), BF16)
    for h in range(N_HEADS):
        v_ref[0, h, 0:HEAD_DIM, :] = jnp.transpose(yv[:, h * HEAD_DIM:(h + 1) * HEAD_DIM]).astype(BF16)
        v_ref[0, h, HEAD_DIM:V_ROWS, :] = ones

    gate_ref[...] = jax.nn.sigmoid(proj(C_G, D_IN))


def _in_proj(x2, g1, w_in, qg, kg, cw, cb, wri, bri, lam, *, bsz, seq):
    t = bsz * seq
    tm = MOBA_BLOCK
    nst = seq // tm
    nrow = -(-nst // SUBLANES) * SUBLANES
    kern = functools.partial(_in_proj_kernel, tiles_per_seq=nst)
    return pl.pallas_call(
        kern,
        grid=(t // tm,),
        in_specs=[
            pl.BlockSpec((tm, D_MODEL), lambda i: (i, 0)),
            _const_spec((1, D_MODEL)),
            _const_spec((D_MODEL, D_IN)),
            _const_spec((1, HEAD_DIM)),
            _const_spec((1, HEAD_DIM)),
            _const_spec((CONV_W, D_RNN)),
            _const_spec((1, D_RNN)),
            _const_spec((RNN_BLOCKS, RNN_BW, 2 * RNN_BW)),
            _const_spec((RNN_BLOCKS, 1, 2 * RNN_BW)),
            _const_spec((1, D_RNN)),
        ],
        out_specs=[
            pl.BlockSpec((tm, D_RNN), lambda i: (i, 0)),
            pl.BlockSpec((1, N_HEADS, tm, 2 * HEAD_DIM), lambda i: (i // nst, 0, i % nst, 0)),
            pl.BlockSpec((1, N_HEADS, tm, 2 * HEAD_DIM), lambda i: (i // nst, 0, i % nst, 0)),
            pl.BlockSpec((1, N_HEADS, V_ROWS, tm), lambda i: (i // nst, 0, 0, i % nst)),
            pl.BlockSpec((tm, 2 * D_MODEL), lambda i: (i, 0)),
        ],
        out_shape=[
            jax.ShapeDtypeStruct((t, D_RNN), BF16),
            jax.ShapeDtypeStruct((bsz, N_HEADS, seq, 2 * HEAD_DIM), BF16),
            jax.ShapeDtypeStruct((bsz, N_HEADS, seq, 2 * HEAD_DIM), BF16),
            jax.ShapeDtypeStruct((bsz, N_HEADS, V_ROWS, seq), BF16),
            jax.ShapeDtypeStruct((t, 2 * D_MODEL), F32),
        ],
        scratch_shapes=[
            pltpu.VMEM((nrow, D_ATTN), F32),
            pltpu.VMEM((tm + SUBLANES, D_RNN), F32),
            pltpu.VMEM((SUBLANES, D_RNN), F32),
        ],
        compiler_params=pltpu.CompilerParams(
            dimension_semantics=("arbitrary",), vmem_limit_bytes=VMEM_LIMIT),
        name="in_proj",
    )(x2, g1, w_in, qg, kg, cw, cb, wri, bri, lam)


def _moba_kernel(q_ref, k_ref, v_ref, o_ref, sa_sc, sb_sc, m_sc, acc_sc, *, nchunk, cblk, hp):
    j = pl.program_id(2)
    bq = MOBA_BLOCK
    kc = cblk * MOBA_BLOCK
    nt_dims = (((1,), (1,)), ((), ()))
    heads = range(hp)
    last = nchunk - 1

    def scores(h, chunk, dst):
        st = pl.multiple_of(chunk * kc, kc)
        dst[h] = lax.dot_general(k_ref[0, h, pl.ds(st, kc), :], q_ref[0, h], nt_dims,
                                 preferred_element_type=F32)

    def accumulate(h, chunk, src):
        st = pl.multiple_of(chunk * kc, kc)
        sc = src[h]
        m_prev = m_sc[h]
        m_new = jnp.maximum(m_prev, jnp.max(sc, axis=0, keepdims=True))
        p = jnp.exp2(sc - m_new).astype(BF16)
        acc_sc[h] = jnp.exp2(m_prev - m_new) * acc_sc[h] + jnp.dot(
            v_ref[0, h, :, pl.ds(st, kc)], p, preferred_element_type=F32)
        m_sc[h] = m_new

    key_i = lax.broadcasted_iota(jnp.int32, (bq, bq), 0)
    qry_i = lax.broadcasted_iota(jnp.int32, (bq, bq), 1)
    own = pl.multiple_of(j * bq, bq)
    for h in heads:
        s = lax.dot_general(k_ref[0, h, pl.ds(own, bq), 0:HEAD_DIM], q_ref[0, h, :, 0:HEAD_DIM],
                            nt_dims, preferred_element_type=F32)
        s = jnp.where(key_i <= qry_i, s, NEG)
        m0 = jnp.max(s, axis=0, keepdims=True)
        m_sc[h] = m0
        acc_sc[h] = jnp.dot(v_ref[0, h, :, pl.ds(own, bq)], jnp.exp2(s - m0).astype(BF16),
                            preferred_element_type=F32)
        scores(h, 0, sa_sc)

    def body(t, carry):
        c0 = 2 * t
        for h in heads:
            scores(h, c0 + 1, sb_sc)
        for h in heads:
            accumulate(h, c0, sa_sc)
        for h in heads:
            scores(h, jnp.minimum(c0 + 2, last), sa_sc)
        for h in heads:
            accumulate(h, c0 + 1, sb_sc)
        return carry

    n_past = lax.div(j + (cblk - 1), cblk)
    lax.fori_loop(0, lax.div(n_past + 1, 2), body, 0)
    for h in heads:
        acc = acc_sc[h]
        o_t = acc[0:HEAD_DIM, :] / acc[HEAD_DIM:HEAD_DIM + 1, :]
        o_ref[0, :, h * HEAD_DIM:(h + 1) * HEAD_DIM] = jnp.transpose(o_t).astype(BF16)


def _moba(q, k, v, *, hp, cblk):
    bsz, nh, seq, _ = q.shape
    nblk = seq // MOBA_BLOCK
    assert nblk % (2 * cblk) == 0 and nh % hp == 0
    kc = cblk * MOBA_BLOCK
    kern = functools.partial(_moba_kernel, nchunk=nblk // cblk, cblk=cblk, hp=hp)
    return pl.pallas_call(
        kern,
        grid=(bsz, nh // hp, nblk),
        in_specs=[
            pl.BlockSpec((1, hp, MOBA_BLOCK, 2 * HEAD_DIM), lambda b, h, j: (b, h, j, 0)),
            pl.BlockSpec((1, hp, seq, 2 * HEAD_DIM), lambda b, h, j: (b, h, 0, 0)),
            pl.BlockSpec((1, hp, V_ROWS, seq), lambda b, h, j: (b, h, 0, 0)),
        ],
        out_specs=pl.BlockSpec((1, MOBA_BLOCK, hp * HEAD_DIM), lambda b, h, j: (b, j, h)),
        out_shape=jax.ShapeDtypeStruct((bsz, seq, D_ATTN), BF16),
        scratch_shapes=[
            pltpu.VMEM((hp, kc, MOBA_BLOCK), F32),
            pltpu.VMEM((hp, kc, MOBA_BLOCK), F32),
            pltpu.VMEM((hp, 1, MOBA_BLOCK), F32),
            pltpu.VMEM((hp, V_ROWS, MOBA_BLOCK), F32),
        ],
        compiler_params=pltpu.CompilerParams(
            dimension_semantics=("arbitrary", "arbitrary", "arbitrary"), vmem_limit_bytes=VMEM_LIMIT),
        name="moba",
    )(q, k, v)


def _merge_kernel(ya_ref, yb_ref, gate_ref, x_ref, pa_ref, pb_ref, wo_ref, o_ref):
    a = jnp.dot(ya_ref[...], pa_ref[...], preferred_element_type=F32)
    b = jnp.dot(yb_ref[...], pb_ref[...], preferred_element_type=F32)
    merged = gate_ref[:, 0:D_MODEL] * a + gate_ref[:, D_MODEL:2 * D_MODEL] * b
    o_ref[...] = x_ref[...] + jnp.dot(merged.astype(BF16), wo_ref[...], preferred_element_type=F32)


def _merge(ya, yb, gates, x2, pa, pb, wo, *, tm):
    t = x2.shape[0]
    row = lambda w: pl.BlockSpec((tm, w), lambda i: (i, 0))
    return pl.pallas_call(
        _merge_kernel,
        grid=(t // tm,),
        in_specs=[row(D_RNN), row(D_ATTN), row(2 * D_MODEL), row(D_MODEL),
                  _const_spec((D_RNN, D_MODEL)), _const_spec((D_ATTN, D_MODEL)),
                  _const_spec((D_MODEL, D_MODEL))],
        out_specs=row(D_MODEL),
        out_shape=jax.ShapeDtypeStruct((t, D_MODEL), F32),
        compiler_params=pltpu.CompilerParams(
            dimension_semantics=("arbitrary",), vmem_limit_bytes=VMEM_LIMIT),
        name="merge",
    )(ya, yb, gates, x2, pa, pb, wo)


def _ffn_kernel(x_ref, g2_ref, wu_ref, wg_ref, cw_ref, cb_ref, wd_ref, o_ref, ubuf, *, tm):
    s = pl.program_id(1)

    @pl.when(s == 0)
    def _():
        ubuf[...] = jnp.zeros_like(ubuf)

    x = x_ref[0]
    hb = _rms(x, g2_ref[...]).astype(BF16)
    ubuf[0:SUBLANES, :] = ubuf[tm:tm + SUBLANES, :]
    ubuf[SUBLANES:SUBLANES + tm, :] = jnp.dot(hb, wu_ref[...], preferred_element_type=F32)
    conv = cb_ref[...]
    for k in range(FFN_CONV_W):
        off = SUBLANES - (FFN_CONV_W - 1) + k
        conv = conv + ubuf[off:off + tm, :] * cw_ref[k:k + 1, :]
    act = _gelu(conv) * jnp.dot(hb, wg_ref[...], preferred_element_type=F32)
    o_ref[0] = x + jnp.dot(act.astype(BF16), wd_ref[...], preferred_element_type=F32)


def _ffn(x3, g2, wu, wg, cw, cb, wd, *, tm):
    bsz, seq, _ = x3.shape
    kern = functools.partial(_ffn_kernel, tm=tm)
    return pl.pallas_call(
        kern,
        grid=(bsz, seq // tm),
        in_specs=[
            pl.BlockSpec((1, tm, D_MODEL), lambda b, s: (b, s, 0)),
            _const_spec((1, D_MODEL)),
            _const_spec((D_MODEL, D_FF)),
            _const_spec((D_MODEL, D_FF)),
            _const_spec((FFN_CONV_W, D_FF)),
            _const_spec((1, D_FF)),
            _const_spec((D_FF, D_MODEL)),
        ],
        out_specs=pl.BlockSpec((1, tm, D_MODEL), lambda b, s: (b, s, 0)),
        out_shape=jax.ShapeDtypeStruct((bsz, seq, D_MODEL), F32),
        scratch_shapes=[pltpu.VMEM((tm + SUBLANES, D_FF), F32)],
        compiler_params=pltpu.CompilerParams(
            dimension_semantics=("arbitrary", "arbitrary"), vmem_limit_bytes=VMEM_LIMIT),
        name="ffn",
    )(x3, g2, wu, wg, cw, cb, wd)


def _tile(seq, want):
    t = min(want, seq)
    assert seq % t == 0 and t % MOBA_BLOCK == 0
    return t


def kernel(x, norm1_g, w_in, conv_w, conv_b, w_r, b_r, w_i, b_i, lru_lambda, q_norm_g, k_norm_g,
           w_proj_rnn, w_proj_attn, w_out, norm2_g, w_up, w_gate, ffn_conv_w, ffn_conv_b, w_down):
    bsz, seq, d = x.shape
    assert d == D_MODEL and seq % MOBA_BLOCK == 0 and seq // MOBA_BLOCK <= LANES
    depth = norm1_g.shape[0]
    t = bsz * seq
    for l in range(depth):
        x2 = x.reshape(t, D_MODEL)
        wri = jnp.concatenate([w_r[l], w_i[l]], axis=-1).astype(BF16)
        bri = jnp.concatenate([b_r[l], b_i[l]], axis=-1)[:, None, :]
        ya, q, k, v, gates = _in_proj(
            x2, norm1_g[l][None], w_in[l].astype(BF16), q_norm_g[l][None], k_norm_g[l][None],
            conv_w[l], conv_b[l][None], wri, bri, lru_lambda[l][None], bsz=bsz, seq=seq)
        yb = _moba(q, k, v, hp=2, cblk=2)
        x1 = _merge(ya.reshape(t, D_RNN), yb.reshape(t, D_ATTN), gates, x2,
                    w_proj_rnn[l].astype(BF16), w_proj_attn[l].astype(BF16), w_out[l].astype(BF16),
                    tm=_tile(seq, 512))
        x = _ffn(x1.reshape(bsz, seq, D_MODEL), norm2_g[l][None], w_up[l].astype(BF16),
                 w_gate[l].astype(BF16), ffn_conv_w[l], ffn_conv_b[l][None], w_down[l].astype(BF16),
                 tm=_tile(seq, 256))
    return x
```

```python
import functools

import jax
import jax.numpy as jnp
from jax import lax
from jax.experimental import pallas as pl
from jax.experimental.pallas import tpu as pltpu

D_MODEL = 1024
D_RNN = 1024
RNN_BLOCKS = 8
RNN_BW = D_RNN // RNN_BLOCKS
CONV_W = 4
LRU_C = 8.0
N_HEADS = 8
HEAD_DIM = 128
D_ATTN = N_HEADS * HEAD_DIM
MOBA_BLOCK = 256
MOBA_TOPK = 3
D_FF = 2816
FFN_CONV_W = 3
EPS = 1e-6
NEG = -1e30
LOG2_E = 1.4426950408889634

LANES = 128
SUBLANES = 8
V_ROWS = HEAD_DIM + 2 * SUBLANES
LONG_BODY = 4
VMEM_LIMIT = 56 * 1024 * 1024

F32 = jnp.float32
BF16 = jnp.bfloat16

C_RNN = 0
C_Q = 2 * D_RNN
C_K = C_Q + D_ATTN
C_V = C_K + D_ATTN
C_G = C_V + D_ATTN
D_IN = C_G + 2 * D_MODEL


def _rms(y, g):
    return y * lax.rsqrt(jnp.mean(y * y, axis=-1, keepdims=True) + EPS) * g


def _gelu(x):
    k1 = -2.0 * 0.7978845608028654 * LOG2_E
    k2 = k1 * 0.044715
    return x * (1.0 / (1.0 + jnp.exp2(x * (k1 + k2 * (x * x)))))


def _const_spec(shape):
    nd = len(shape)
    return pl.BlockSpec(shape, lambda *_: (0,) * nd, pipeline_mode=pl.Buffered(1))


def _lru_scan(a, b, h0):
    rows, width = a.shape
    groups = rows // SUBLANES
    a = a.reshape(groups, SUBLANES, width)
    b = b.reshape(groups, SUBLANES, width)
    sub = lax.broadcasted_iota(jnp.int32, a.shape, 1)
    for d in (1, 2, 4):
        a_sh = jnp.where(sub >= d, pltpu.roll(a, d, 1), 1.0)
        b_sh = jnp.where(sub >= d, pltpu.roll(b, d, 1), 0.0)
        b = a * b_sh + b
        a = a * a_sh
    out = []
    for g in range(groups):
        hg = b[g] + a[g] * h0
        h0 = hg[SUBLANES - 1:SUBLANES]
        out.append(hg)
    return jnp.concatenate(out, axis=0), h0


def _in_proj_kernel(x_ref, g1_ref, w_ref, qg_ref, kg_ref, cw_ref, cb_ref, wri_ref, bri_ref, lam_ref,
                    ya_ref, q_ref, k_ref, v_ref, gate_ref, km_sc, xbuf, hst, hb_sc, *, tiles_per_seq):
    tm = MOBA_BLOCK
    j = pl.program_id(0) % tiles_per_seq
    nt_dims = (((1,), (1,)), ((), ()))

    @pl.when(j == 0)
    def _():
        km_sc[...] = jnp.zeros_like(km_sc)
        xbuf[...] = jnp.zeros_like(xbuf)
        hst[...] = jnp.zeros_like(hst)

    hb_sc[...] = _rms(x_ref[...], g1_ref[...]).astype(BF16)

    def proj(c0, c1):
        return jnp.dot(hb_sc[...], w_ref[:, c0:c1], preferred_element_type=F32)

    rnn = proj(C_RNN, C_Q)
    xbuf[0:SUBLANES, :] = xbuf[tm:tm + SUBLANES, :]
    xbuf[SUBLANES:SUBLANES + tm, :] = rnn[:, 0:D_RNN]
    for n in range(RNN_BLOCKS):
        cols = slice(n * RNN_BW, (n + 1) * RNN_BW)
        xa = cb_ref[:, cols]
        for k in range(CONV_W):
            off = SUBLANES - (CONV_W - 1) + k
            xa = xa + xbuf[off:off + tm, cols] * cw_ref[k:k + 1, cols]
        z = jnp.dot(xa.astype(BF16), wri_ref[n], preferred_element_type=F32) + bri_ref[n]
        r = jax.nn.sigmoid(z[:, 0:RNN_BW])
        gi = jax.nn.sigmoid(z[:, RNN_BW:2 * RNN_BW])
        a = jnp.exp(r * (-LRU_C * jax.nn.softplus(-lam_ref[:, cols])))
        w = 1.0 - a * a
        b = jnp.where(w > 0.0, w * lax.rsqrt(w), 0.0) * (gi * xa)
        hseq, hlast = _lru_scan(a, b, hst[0:1, cols])
        hst[0:1, cols] = hlast
        ya_ref[:, cols] = (hseq * _gelu(rnn[:, D_RNN + n * RNN_BW:D_RNN + (n + 1) * RNN_BW])).astype(BF16)

    nrow = km_sc.shape[0]
    blk = lax.broadcasted_iota(jnp.int32, (nrow, tm), 0)
    blk_f = blk.astype(F32)
    scale = HEAD_DIM ** -0.5 * LOG2_E
    yq = proj(C_Q, C_K)
    for h in range(N_HEADS):
        cols = slice(h * HEAD_DIM, (h + 1) * HEAD_DIM)
        y = _rms(yq[:, cols], qg_ref[...]) * scale
        q_ref[0, h, :, 0:HEAD_DIM] = y.astype(BF16)
        gate_t = lax.dot_general(km_sc[:, cols], y, nt_dims,
                                 precision=lax.Precision.HIGHEST, preferred_element_type=F32)
        g = jnp.where(blk < j, gate_t, NEG)
        sel = jnp.zeros((nrow, tm), dtype=jnp.bool_)
        for slot in range(MOBA_TOPK):
            gmax = jnp.max(g, axis=0, keepdims=True)
            idx = jnp.min(jnp.where(g == gmax, blk_f, float(LANES)), axis=0, keepdims=True)
            hit = blk_f == idx
            sel = jnp.logical_or(sel, jnp.logical_and(hit, slot < j))
            g = jnp.where(hit, -3e38, g)
        negsel_t = jnp.where(sel, 0.0, NEG)
        if nrow < LANES:
            negsel_t = jnp.concatenate([negsel_t, jnp.zeros((LANES - nrow, tm), F32)], axis=0)
        q_ref[0, h, :, HEAD_DIM:2 * HEAD_DIM] = jnp.transpose(negsel_t).astype(BF16)

    lane = lax.broadcasted_iota(jnp.int32, (tm, LANES), 1)
    onehot = jnp.where(lane == j, 1.0, 0.0).astype(BF16)
    yk = proj(C_K, C_V)
    for h in range(N_HEADS):
        cols = slice(h * HEAD_DIM, (h + 1) * HEAD_DIM)
        y = _rms(yk[:, cols], kg_ref[...])
        k_ref[0, h, :, 0:HEAD_DIM] = y.astype(BF16)
        k_ref[0, h, :, HEAD_DIM:2 * HEAD_DIM] = onehot
        km_sc[:, cols] = jnp.where(blk[:, 0:HEAD_DIM] == j, jnp.mean(y, axis=0, keepdims=True), km_sc[:, cols])

    yv = proj(C_V, C_G)
    ones = jnp.ones((V_ROWS - HEAD_DIM, tm), BF16)
    for h in range(N_HEADS):
        v_ref[0, h, 0:HEAD_DIM, :] = jnp.transpose(yv[:, h * HEAD_DIM:(h + 1) * HEAD_DIM]).astype(BF16)
        v_ref[0, h, HEAD_DIM:V_ROWS, :] = ones

    gate_ref[...] = jax.nn.sigmoid(proj(C_G, D_IN))


def _in_proj(x2, g1, w_in, qg, kg, cw, cb, wri, bri, lam, *, bsz, seq):
    t = bsz * seq
    tm = MOBA_BLOCK
    nst = seq // tm
    nrow = -(-nst // SUBLANES) * SUBLANES
    kern = functools.partial(_in_proj_kernel, tiles_per_seq=nst)
    return pl.pallas_call(
        kern,
        grid=(t // tm,),
        in_specs=[
            pl.BlockSpec((tm, D_MODEL), lambda i: (i, 0)),
            _const_spec((1, D_MODEL)),
            _const_spec((D_MODEL, D_IN)),
            _const_spec((1, HEAD_DIM)),
            _const_spec((1, HEAD_DIM)),
            _const_spec((CONV_W, D_RNN)),
            _const_spec((1, D_RNN)),
            _const_spec((RNN_BLOCKS, RNN_BW, 2 * RNN_BW)),
            _const_spec((RNN_BLOCKS, 1, 2 * RNN_BW)),
            _const_spec((1, D_RNN)),
        ],
        out_specs=[
            pl.BlockSpec((tm, D_RNN), lambda i: (i, 0)),
            pl.BlockSpec((1, N_HEADS, tm, 2 * HEAD_DIM), lambda i: (i // nst, 0, i % nst, 0)),
            pl.BlockSpec((1, N_HEADS, tm, 2 * HEAD_DIM), lambda i: (i // nst, 0, i % nst, 0)),
            pl.BlockSpec((1, N_HEADS, V_ROWS, tm), lambda i: (i // nst, 0, 0, i % nst)),
            pl.BlockSpec((tm, 2 * D_MODEL), lambda i: (i, 0)),
        ],
        out_shape=[
            jax.ShapeDtypeStruct((t, D_RNN), BF16),
            jax.ShapeDtypeStruct((bsz, N_HEADS, seq, 2 * HEAD_DIM), BF16),
            jax.ShapeDtypeStruct((bsz, N_HEADS, seq, 2 * HEAD_DIM), BF16),
            jax.ShapeDtypeStruct((bsz, N_HEADS, V_ROWS, seq), BF16),
            jax.ShapeDtypeStruct((t, 2 * D_MODEL), F32),
        ],
        scratch_shapes=[
            pltpu.VMEM((nrow, D_ATTN), F32),
            pltpu.VMEM((tm + SUBLANES, D_RNN), F32),
            pltpu.VMEM((SUBLANES, D_RNN), F32),
            pltpu.VMEM((tm, D_MODEL), BF16),
        ],
        compiler_params=pltpu.CompilerParams(
            dimension_semantics=("arbitrary",), vmem_limit_bytes=VMEM_LIMIT),
        name="in_proj",
    )(x2, g1, w_in, qg, kg, cw, cb, wri, bri, lam)


def _moba_kernel(q_ref, k_ref, v_ref, o_ref, sa_sc, sb_sc, m_sc, acc_sc, *, nchunk, cblk, hp):
    j = pl.program_id(2)
    bq = MOBA_BLOCK
    kc = cblk * MOBA_BLOCK
    nt_dims = (((1,), (1,)), ((), ()))
    heads = range(hp)
    last = nchunk - 1

    def scores(h, chunk, dst):
        st = pl.multiple_of(chunk * kc, kc)
        dst[h] = lax.dot_general(k_ref[0, h, pl.ds(st, kc), :], q_ref[0, h], nt_dims,
                                 preferred_element_type=F32)

    def accumulate(h, chunk, src):
        st = pl.multiple_of(chunk * kc, kc)
        sc = src[h]
        m_prev = m_sc[h]
        m_new = jnp.maximum(m_prev, jnp.max(sc, axis=0, keepdims=True))
        p = jnp.exp2(sc - m_new).astype(BF16)
        acc_sc[h] = jnp.exp2(m_prev - m_new) * acc_sc[h] + jnp.dot(
            v_ref[0, h, :, pl.ds(st, kc)], p, preferred_element_type=F32)
        m_sc[h] = m_new

    key_i = lax.broadcasted_iota(jnp.int32, (bq, bq), 0)
    qry_i = lax.broadcasted_iota(jnp.int32, (bq, bq), 1)
    own = pl.multiple_of(j * bq, bq)
    for h in heads:
        s = lax.dot_general(k_ref[0, h, pl.ds(own, bq), 0:HEAD_DIM], q_ref[0, h, :, 0:HEAD_DIM],
                            nt_dims, preferred_element_type=F32)
        s = jnp.where(key_i <= qry_i, s, NEG)
        m0 = jnp.max(s, axis=0, keepdims=True)
        m_sc[h] = m0
        acc_sc[h] = jnp.dot(v_ref[0, h, :, pl.ds(own, bq)], jnp.exp2(s - m0).astype(BF16),
                            preferred_element_type=F32)
        scores(h, 0, sa_sc)

    def run(first, trips, unroll):
        def body(t, carry):
            c0 = first + unroll * t
            for u in range(unroll):
                cur, nxt = (sa_sc, sb_sc) if u % 2 == 0 else (sb_sc, sa_sc)
                for h in heads:
                    scores(h, jnp.minimum(c0 + u + 1, last), nxt)
                for h in heads:
                    accumulate(h, c0 + u, cur)
            return carry

        lax.fori_loop(0, trips, body, 0)

    n_past = lax.div(j + (cblk - 1), cblk)
    n_long = lax.div(n_past, LONG_BODY)
    run(0, n_long, LONG_BODY)
    run(n_long * LONG_BODY, lax.div(n_past - n_long * LONG_BODY + 1, 2), 2)
    for h in heads:
        acc = acc_sc[h]
        o_t = acc[0:HEAD_DIM, :] / acc[HEAD_DIM:HEAD_DIM + 1, :]
        o_ref[0, :, h * HEAD_DIM:(h + 1) * HEAD_DIM] = jnp.transpose(o_t).astype(BF16)


def _moba(q, k, v, *, hp, cblk):
    bsz, nh, seq, _ = q.shape
    nblk = seq // MOBA_BLOCK
    assert nblk % (2 * cblk) == 0 and nh % hp == 0
    kc = cblk * MOBA_BLOCK
    kern = functools.partial(_moba_kernel, nchunk=nblk // cblk, cblk=cblk, hp=hp)
    return pl.pallas_call(
        kern,
        grid=(bsz, nh // hp, nblk),
        in_specs=[
            pl.BlockSpec((1, hp, MOBA_BLOCK, 2 * HEAD_DIM), lambda b, h, j: (b, h, j, 0)),
            pl.BlockSpec((1, hp, seq, 2 * HEAD_DIM), lambda b, h, j: (b, h, 0, 0)),
            pl.BlockSpec((1, hp, V_ROWS, seq), lambda b, h, j: (b, h, 0, 0)),
        ],
        out_specs=pl.BlockSpec((1, MOBA_BLOCK, hp * HEAD_DIM), lambda b, h, j: (b, j, h)),
        out_shape=jax.ShapeDtypeStruct((bsz, seq, D_ATTN), BF16),
        scratch_shapes=[
            pltpu.VMEM((hp, kc, MOBA_BLOCK), F32),
            pltpu.VMEM((hp, kc, MOBA_BLOCK), F32),
            pltpu.VMEM((hp, 1, MOBA_BLOCK), F32),
            pltpu.VMEM((hp, V_ROWS, MOBA_BLOCK), F32),
        ],
        compiler_params=pltpu.CompilerParams(
            dimension_semantics=("arbitrary", "arbitrary", "arbitrary"), vmem_limit_bytes=VMEM_LIMIT),
        name="moba",
    )(q, k, v)


def _merge_kernel(ya_ref, yb_ref, gate_ref, x_ref, pa_ref, pb_ref, wo_ref, o_ref):
    a = jnp.dot(ya_ref[...], pa_ref[...], preferred_element_type=F32)
    b = jnp.dot(yb_ref[...], pb_ref[...], preferred_element_type=F32)
    merged = gate_ref[:, 0:D_MODEL] * a + gate_ref[:, D_MODEL:2 * D_MODEL] * b
    o_ref[...] = x_ref[...] + jnp.dot(merged.astype(BF16), wo_ref[...], preferred_element_type=F32)


def _merge(ya, yb, gates, x2, pa, pb, wo, *, tm):
    t = x2.shape[0]
    row = lambda w: pl.BlockSpec((tm, w), lambda i: (i, 0))
    return pl.pallas_call(
        _merge_kernel,
        grid=(t // tm,),
        in_specs=[row(D_RNN), row(D_ATTN), row(2 * D_MODEL), row(D_MODEL),
                  _const_spec((D_RNN, D_MODEL)), _const_spec((D_ATTN, D_MODEL)),
                  _const_spec((D_MODEL, D_MODEL))],
        out_specs=row(D_MODEL),
        out_shape=jax.ShapeDtypeStruct((t, D_MODEL), F32),
        compiler_params=pltpu.CompilerParams(
            dimension_semantics=("arbitrary",), vmem_limit_bytes=VMEM_LIMIT),
        name="merge",
    )(ya, yb, gates, x2, pa, pb, wo)


def _ffn_kernel(x_ref, g2_ref, wu_ref, wg_ref, cw_ref, cb_ref, wd_ref, o_ref, ubuf, *, tm):
    s = pl.program_id(1)

    @pl.when(s == 0)
    def _():
        ubuf[...] = jnp.zeros_like(ubuf)

    x = x_ref[0]
    hb = _rms(x, g2_ref[...]).astype(BF16)
    ubuf[0:SUBLANES, :] = ubuf[tm:tm + SUBLANES, :]
    ubuf[SUBLANES:SUBLANES + tm, :] = jnp.dot(hb, wu_ref[...], preferred_element_type=F32)
    conv = cb_ref[...]
    for k in range(FFN_CONV_W):
        off = SUBLANES - (FFN_CONV_W - 1) + k
        conv = conv + ubuf[off:off + tm, :] * cw_ref[k:k + 1, :]
    act = _gelu(conv) * jnp.dot(hb, wg_ref[...], preferred_element_type=F32)
    o_ref[0] = x + jnp.dot(act.astype(BF16), wd_ref[...], preferred_element_type=F32)


def _ffn(x3, g2, wu, wg, cw, cb, wd, *, tm):
    bsz, seq, _ = x3.shape
    kern = functools.partial(_ffn_kernel, tm=tm)
    return pl.pallas_call(
        kern,
        grid=(bsz, seq // tm),
        in_specs=[
            pl.BlockSpec((1, tm, D_MODEL), lambda b, s: (b, s, 0)),
            _const_spec((1, D_MODEL)),
            _const_spec((D_MODEL, D_FF)),
            _const_spec((D_MODEL, D_FF)),
            _const_spec((FFN_CONV_W, D_FF)),
            _const_spec((1, D_FF)),
            _const_spec((D_FF, D_MODEL)),
        ],
        out_specs=pl.BlockSpec((1, tm, D_MODEL), lambda b, s: (b, s, 0)),
        out_shape=jax.ShapeDtypeStruct((bsz, seq, D_MODEL), F32),
        scratch_shapes=[pltpu.VMEM((tm + SUBLANES, D_FF), F32)],
        compiler_params=pltpu.CompilerParams(
            dimension_semantics=("arbitrary", "arbitrary"), vmem_limit_bytes=VMEM_LIMIT),
        name="ffn",
    )(x3, g2, wu, wg, cw, cb, wd)


def _tile(seq, want):
    t = min(want, seq)
    assert seq % t == 0 and t % MOBA_BLOCK == 0
    return t


def kernel(x, norm1_g, w_in, conv_w, conv_b, w_r, b_r, w_i, b_i, lru_lambda, q_norm_g, k_norm_g,
           w_proj_rnn, w_proj_attn, w_out, norm2_g, w_up, w_gate, ffn_conv_w, ffn_conv_b, w_down):
    bsz, seq, d = x.shape
    assert d == D_MODEL and seq % MOBA_BLOCK == 0 and seq // MOBA_BLOCK <= LANES
    depth = norm1_g.shape[0]
    t = bsz * seq
    for l in range(depth):
        x2 = x.reshape(t, D_MODEL)
        wri = jnp.concatenate([w_r[l], w_i[l]], axis=-1).astype(BF16)
        bri = jnp.concatenate([b_r[l], b_i[l]], axis=-1)[:, None, :]
        ya, q, k, v, gates = _in_proj(
            x2, norm1_g[l][None], w_in[l].astype(BF16), q_norm_g[l][None], k_norm_g[l][None],
            conv_w[l], conv_b[l][None], wri, bri, lru_lambda[l][None], bsz=bsz, seq=seq)
        yb = _moba(q, k, v, hp=2, cblk=2)
        x1 = _merge(ya.reshape(t, D_RNN), yb.reshape(t, D_ATTN), gates, x2,
                    w_proj_rnn[l].astype(BF16), w_proj_attn[l].astype(BF16), w_out[l].astype(BF16),
                    tm=_tile(seq, 512))
        x = _ffn(x1.reshape(bsz, seq, D_MODEL), norm2_g[l][None], w_up[l].astype(BF16),
                 w_gate[l].astype(BF16), ffn_conv_w[l], ffn_conv_b[l][None], w_down[l].astype(BF16),
                 tm=_tile(seq, 256))
    return x
```

```python
import functools

import jax
import jax.numpy as jnp
from jax import lax
from jax.experimental import pallas as pl
from jax.experimental.pallas import tpu as pltpu

D_MODEL = 1024
D_RNN = 1024
RNN_BLOCKS = 8
RNN_BW = D_RNN // RNN_BLOCKS
CONV_W = 4
LRU_C = 8.0
N_HEADS = 8
HEAD_DIM = 128
D_ATTN = N_HEADS * HEAD_DIM
MOBA_BLOCK = 256
MOBA_TOPK = 3
D_FF = 2816
FFN_CONV_W = 3
EPS = 1e-6
NEG = -1e30
LOG2_E = 1.4426950408889634

LANES = 128
SUBLANES = 8
V_ROWS = HEAD_DIM + 2 * SUBLANES
MOBA_TILE = 2 * MOBA_BLOCK
LONG_BODY = 4
VMEM_LIMIT = 56 * 1024 * 1024

F32 = jnp.float32
BF16 = jnp.bfloat16

C_RNN = 0
C_Q = 2 * D_RNN
C_K = C_Q + D_ATTN
C_V = C_K + D_ATTN
C_G = C_V + D_ATTN
D_IN = C_G + 2 * D_MODEL


def _rms(y, g):
    return y * lax.rsqrt(jnp.mean(y * y, axis=-1, keepdims=True) + EPS) * g


def _gelu(x):
    k1 = -2.0 * 0.7978845608028654 * LOG2_E
    k2 = k1 * 0.044715
    return x * (1.0 / (1.0 + jnp.exp2(x * (k1 + k2 * (x * x)))))


def _const_spec(shape):
    nd = len(shape)
    return pl.BlockSpec(shape, lambda *_: (0,) * nd, pipeline_mode=pl.Buffered(1))


def _lru_scan(a, b, h0):
    rows, width = a.shape
    groups = rows // SUBLANES
    a = a.reshape(groups, SUBLANES, width)
    b = b.reshape(groups, SUBLANES, width)
    sub = lax.broadcasted_iota(jnp.int32, a.shape, 1)
    for d in (1, 2, 4):
        a_sh = jnp.where(sub >= d, pltpu.roll(a, d, 1), 1.0)
        b_sh = jnp.where(sub >= d, pltpu.roll(b, d, 1), 0.0)
        b = a * b_sh + b
        a = a * a_sh
    out = []
    for g in range(groups):
        hg = b[g] + a[g] * h0
        h0 = hg[SUBLANES - 1:SUBLANES]
        out.append(hg)
    return jnp.concatenate(out, axis=0), h0


def _in_proj_kernel(x_ref, g1_ref, w_ref, qg_ref, kg_ref, cw_ref, cb_ref, wri_ref, bri_ref, lam_ref,
                    ya_ref, q_ref, k_ref, v_ref, gate_ref, km_sc, xbuf, hst, hb_sc, *, tiles_per_seq):
    tm = MOBA_BLOCK
    j = pl.program_id(0) % tiles_per_seq
    nt_dims = (((1,), (1,)), ((), ()))

    @pl.when(j == 0)
    def _():
        km_sc[...] = jnp.zeros_like(km_sc)
        xbuf[...] = jnp.zeros_like(xbuf)
        hst[...] = jnp.zeros_like(hst)

    hb_sc[...] = _rms(x_ref[...], g1_ref[...]).astype(BF16)
    pw = D_RNN

    def proj(c0):
        return jnp.dot(hb_sc[...], w_ref[:, c0:c0 + pw], preferred_element_type=F32)

    def rnn_unit(p):
        pcols = slice(p * pw, (p + 1) * pw)
        g_rnn = proj(C_RNN + D_RNN + p * pw)
        xbuf[0:SUBLANES, pcols] = xbuf[tm:tm + SUBLANES, pcols]
        xbuf[SUBLANES:SUBLANES + tm, pcols] = proj(C_RNN + p * pw)
        for i in range(pw // RNN_BW):
            n = p * (pw // RNN_BW) + i
            cols = slice(n * RNN_BW, (n + 1) * RNN_BW)
            xa = cb_ref[:, cols]
            for k in range(CONV_W):
                off = SUBLANES - (CONV_W - 1) + k
                xa = xa + xbuf[off:off + tm, cols] * cw_ref[k:k + 1, cols]
            z = jnp.dot(xa.astype(BF16), wri_ref[n], preferred_element_type=F32) + bri_ref[n]
            r = jax.nn.sigmoid(z[:, 0:RNN_BW])
            gi = jax.nn.sigmoid(z[:, RNN_BW:2 * RNN_BW])
            a = jnp.exp(r * (-LRU_C * jax.nn.softplus(-lam_ref[:, cols])))
            w = 1.0 - a * a
            b = jnp.where(w > 0.0, w * lax.rsqrt(w), 0.0) * (gi * xa)
            hseq, hlast = _lru_scan(a, b, hst[0:1, cols])
            hst[0:1, cols] = hlast
            ya_ref[:, cols] = (hseq * _gelu(g_rnn[:, i * RNN_BW:(i + 1) * RNN_BW])).astype(BF16)

    nrow = km_sc.shape[0]
    blk = lax.broadcasted_iota(jnp.int32, (nrow, tm), 0)
    blk_f = blk.astype(F32)
    scale = HEAD_DIM ** -0.5 * LOG2_E

    def q_unit(p):
        yq = proj(C_Q + p * pw)
        for i in range(pw // HEAD_DIM):
            h = p * (pw // HEAD_DIM) + i
            cols = slice(h * HEAD_DIM, (h + 1) * HEAD_DIM)
            y = _rms(yq[:, i * HEAD_DIM:(i + 1) * HEAD_DIM], qg_ref[...]) * scale
            q_ref[0, h, :, 0:HEAD_DIM] = y.astype(BF16)
            gate_t = lax.dot_general(km_sc[:, cols], y, nt_dims,
                                     precision=lax.Precision.HIGHEST, preferred_element_type=F32)
            g = jnp.where(blk < j, gate_t, NEG)
            sel = jnp.zeros((nrow, tm), dtype=jnp.bool_)
            for slot in range(MOBA_TOPK):
                gmax = jnp.max(g, axis=0, keepdims=True)
                idx = jnp.min(jnp.where(g == gmax, blk_f, float(LANES)), axis=0, keepdims=True)
                hit = blk_f == idx
                sel = jnp.logical_or(sel, jnp.logical_and(hit, slot < j))
                g = jnp.where(hit, -3e38, g)
            negsel_t = jnp.where(jnp.logical_or(sel, blk == j), 0.0, NEG)
            if nrow < LANES:
                negsel_t = jnp.concatenate([negsel_t, jnp.zeros((LANES - nrow, tm), F32)], axis=0)
            q_ref[0, h, :, HEAD_DIM:2 * HEAD_DIM] = jnp.transpose(negsel_t).astype(BF16)

    lane = lax.broadcasted_iota(jnp.int32, (tm, LANES), 1)
    onehot = jnp.where(lane == j, 1.0, 0.0).astype(BF16)

    def k_unit(p):
        yk = proj(C_K + p * pw)
        for i in range(pw // HEAD_DIM):
            h = p * (pw // HEAD_DIM) + i
            cols = slice(h * HEAD_DIM, (h + 1) * HEAD_DIM)
            y = _rms(yk[:, i * HEAD_DIM:(i + 1) * HEAD_DIM], kg_ref[...])
            k_ref[0, h, :, 0:HEAD_DIM] = y.astype(BF16)
            k_ref[0, h, :, HEAD_DIM:2 * HEAD_DIM] = onehot
            km_sc[:, cols] = jnp.where(blk[:, 0:HEAD_DIM] == j, jnp.mean(y, axis=0, keepdims=True),
                                       km_sc[:, cols])

    ones = jnp.ones((V_ROWS - HEAD_DIM, tm), BF16)

    def v_unit(p):
        yv = proj(C_V + p * pw)
        for i in range(pw // HEAD_DIM):
            h = p * (pw // HEAD_DIM) + i
            v_ref[0, h, 0:HEAD_DIM, :] = jnp.transpose(yv[:, i * HEAD_DIM:(i + 1) * HEAD_DIM]).astype(BF16)
            v_ref[0, h, HEAD_DIM:V_ROWS, :] = ones

    def gate_unit(c):
        gate_ref[:, c * pw:(c + 1) * pw] = jax.nn.sigmoid(proj(C_G + c * pw))

    for p in range(D_RNN // pw):
        rnn_unit(p)
    for p in range(D_ATTN // pw):
        q_unit(p)
    for p in range(D_ATTN // pw):
        k_unit(p)
    for p in range(D_ATTN // pw):
        v_unit(p)
    for c in range(2 * D_MODEL // pw):
        gate_unit(c)


def _in_proj(x2, g1, w_in, qg, kg, cw, cb, wri, bri, lam, *, bsz, seq):
    t = bsz * seq
    tm = MOBA_BLOCK
    nst = seq // tm
    nrow = -(-nst // SUBLANES) * SUBLANES
    kern = functools.partial(_in_proj_kernel, tiles_per_seq=nst)
    return pl.pallas_call(
        kern,
        grid=(t // tm,),
        in_specs=[
            pl.BlockSpec((tm, D_MODEL), lambda i: (i, 0)),
            _const_spec((1, D_MODEL)),
            _const_spec((D_MODEL, D_IN)),
            _const_spec((1, HEAD_DIM)),
            _const_spec((1, HEAD_DIM)),
            _const_spec((CONV_W, D_RNN)),
            _const_spec((1, D_RNN)),
            _const_spec((RNN_BLOCKS, RNN_BW, 2 * RNN_BW)),
            _const_spec((RNN_BLOCKS, 1, 2 * RNN_BW)),
            _const_spec((1, D_RNN)),
        ],
        out_specs=[
            pl.BlockSpec((tm, D_RNN), lambda i: (i, 0)),
            pl.BlockSpec((1, N_HEADS, tm, 2 * HEAD_DIM), lambda i: (i // nst, 0, i % nst, 0)),
            pl.BlockSpec((1, N_HEADS, tm, 2 * HEAD_DIM), lambda i: (i // nst, 0, i % nst, 0)),
            pl.BlockSpec((1, N_HEADS, V_ROWS, tm), lambda i: (i // nst, 0, 0, i % nst)),
            pl.BlockSpec((tm, 2 * D_MODEL), lambda i: (i, 0)),
        ],
        out_shape=[
            jax.ShapeDtypeStruct((t, D_RNN), BF16),
            jax.ShapeDtypeStruct((bsz, N_HEADS, seq, 2 * HEAD_DIM), BF16),
            jax.ShapeDtypeStruct((bsz, N_HEADS, seq, 2 * HEAD_DIM), BF16),
            jax.ShapeDtypeStruct((bsz, N_HEADS, V_ROWS, seq), BF16),
            jax.ShapeDtypeStruct((t, 2 * D_MODEL), F32),
        ],
        scratch_shapes=[
            pltpu.VMEM((nrow, D_ATTN), F32),
            pltpu.VMEM((tm + SUBLANES, D_RNN), F32),
            pltpu.VMEM((SUBLANES, D_RNN), F32),
            pltpu.VMEM((tm, D_MODEL), BF16),
        ],
        compiler_params=pltpu.CompilerParams(
            dimension_semantics=("arbitrary",), vmem_limit_bytes=VMEM_LIMIT),
        name="in_proj",
    )(x2, g1, w_in, qg, kg, cw, cb, wri, bri, lam)


def _moba_kernel(q_ref, k_ref, v_ref, o_ref, sa_sc, sb_sc, m_sc, acc_sc, *, hp):
    t = pl.program_id(2)
    kc = MOBA_TILE
    nt_dims = (((1,), (1,)), ((), ()))
    heads = range(hp)

    def scores(h, chunk, dst):
        st = pl.multiple_of(chunk * kc, kc)
        dst[h] = lax.dot_general(k_ref[0, h, pl.ds(st, kc), :], q_ref[0, h], nt_dims,
                                 preferred_element_type=F32)

    def accumulate(h, chunk, src, causal=False):
        st = pl.multiple_of(chunk * kc, kc)
        sc = src[h]
        if causal:
            key_i = lax.broadcasted_iota(jnp.int32, sc.shape, 0)
            qry_i = lax.broadcasted_iota(jnp.int32, sc.shape, 1)
            sc = jnp.where(key_i <= qry_i, sc, NEG)
        m_prev = m_sc[h]
        m_new = jnp.maximum(m_prev, jnp.max(sc, axis=0, keepdims=True))
        p = jnp.exp2(sc - m_new).astype(BF16)
        acc_sc[h] = jnp.exp2(m_prev - m_new) * acc_sc[h] + jnp.dot(
            v_ref[0, h, :, pl.ds(st, kc)], p, preferred_element_type=F32)
        m_sc[h] = m_new

    m_sc[...] = jnp.full_like(m_sc, NEG)
    acc_sc[...] = jnp.zeros_like(acc_sc)
    for h in heads:
        scores(h, 0, sa_sc)

    def step(c, cur, nxt):
        for h in heads:
            scores(h, c + 1, nxt)
        for h in heads:
            accumulate(h, c, cur)

    def run(first, trips, unroll):
        def body(i, carry):
            for u in range(unroll):
                cur, nxt = (sa_sc, sb_sc) if u % 2 == 0 else (sb_sc, sa_sc)
                step(first + unroll * i + u, cur, nxt)
            return carry

        lax.fori_loop(0, trips, body, 0)

    n_long = lax.div(t, LONG_BODY)
    rest = t - n_long * LONG_BODY
    run(0, n_long, LONG_BODY)
    run(n_long * LONG_BODY, lax.div(rest, 2), 2)

    @pl.when(rest % 2 == 1)
    def _():
        step(t - 1, sa_sc, sb_sc)
        for h in heads:
            accumulate(h, t, sb_sc, causal=True)

    @pl.when(rest % 2 == 0)
    def _():
        for h in heads:
            accumulate(h, t, sa_sc, causal=True)

    for h in heads:
        acc = acc_sc[h]
        o_t = acc[0:HEAD_DIM, :] / acc[HEAD_DIM:HEAD_DIM + 1, :]
        o_ref[0, :, h * HEAD_DIM:(h + 1) * HEAD_DIM] = jnp.transpose(o_t).astype(BF16)


def _moba(q, k, v, *, hp):
    bsz, nh, seq, _ = q.shape
    assert seq % MOBA_TILE == 0 and nh % hp == 0
    kc = MOBA_TILE
    kern = functools.partial(_moba_kernel, hp=hp)
    return pl.pallas_call(
        kern,
        grid=(bsz, nh // hp, seq // MOBA_TILE),
        in_specs=[
            pl.BlockSpec((1, hp, MOBA_TILE, 2 * HEAD_DIM), lambda b, h, j: (b, h, j, 0)),
            pl.BlockSpec((1, hp, seq, 2 * HEAD_DIM), lambda b, h, j: (b, h, 0, 0)),
            pl.BlockSpec((1, hp, V_ROWS, seq), lambda b, h, j: (b, h, 0, 0)),
        ],
        out_specs=pl.BlockSpec((1, MOBA_TILE, hp * HEAD_DIM), lambda b, h, j: (b, j, h)),
        out_shape=jax.ShapeDtypeStruct((bsz, seq, D_ATTN), BF16),
        scratch_shapes=[
            pltpu.VMEM((hp, kc, MOBA_TILE), F32),
            pltpu.VMEM((hp, kc, MOBA_TILE), F32),
            pltpu.VMEM((hp, 1, MOBA_TILE), F32),
            pltpu.VMEM((hp, V_ROWS, MOBA_TILE), F32),
        ],
        compiler_params=pltpu.CompilerParams(
            dimension_semantics=("arbitrary", "arbitrary", "arbitrary"), vmem_limit_bytes=VMEM_LIMIT),
        name="moba",
    )(q, k, v)


def _merge_kernel(ya_ref, yb_ref, gate_ref, x_ref, pa_ref, pb_ref, wo_ref, o_ref):
    a = jnp.dot(ya_ref[...], pa_ref[...], preferred_element_type=F32)
    b = jnp.dot(yb_ref[...], pb_ref[...], preferred_element_type=F32)
    merged = gate_ref[:, 0:D_MODEL] * a + gate_ref[:, D_MODEL:2 * D_MODEL] * b
    o_ref[...] = x_ref[...] + jnp.dot(merged.astype(BF16), wo_ref[...], preferred_element_type=F32)


def _merge(ya, yb, gates, x2, pa, pb, wo, *, tm):
    t = x2.shape[0]
    row = lambda w: pl.BlockSpec((tm, w), lambda i: (i, 0))
    return pl.pallas_call(
        _merge_kernel,
        grid=(t // tm,),
        in_specs=[row(D_RNN), row(D_ATTN), row(2 * D_MODEL), row(D_MODEL),
                  _const_spec((D_RNN, D_MODEL)), _const_spec((D_ATTN, D_MODEL)),
                  _const_spec((D_MODEL, D_MODEL))],
        out_specs=row(D_MODEL),
        out_shape=jax.ShapeDtypeStruct((t, D_MODEL), F32),
        compiler_params=pltpu.CompilerParams(
            dimension_semantics=("arbitrary",), vmem_limit_bytes=VMEM_LIMIT),
        name="merge",
    )(ya, yb, gates, x2, pa, pb, wo)


def _ffn_kernel(x_ref, g2_ref, wu_ref, wg_ref, cw_ref, cb_ref, wd_ref, o_ref, ubuf, *, tm):
    s = pl.program_id(1)

    @pl.when(s == 0)
    def _():
        ubuf[...] = jnp.zeros_like(ubuf)

    x = x_ref[0]
    hb = _rms(x, g2_ref[...]).astype(BF16)
    ubuf[0:SUBLANES, :] = ubuf[tm:tm + SUBLANES, :]
    ubuf[SUBLANES:SUBLANES + tm, :] = jnp.dot(hb, wu_ref[...], preferred_element_type=F32)
    conv = cb_ref[...]
    for k in range(FFN_CONV_W):
        off = SUBLANES - (FFN_CONV_W - 1) + k
        conv = conv + ubuf[off:off + tm, :] * cw_ref[k:k + 1, :]
    act = _gelu(conv) * jnp.dot(hb, wg_ref[...], preferred_element_type=F32)
    o_ref[0] = x + jnp.dot(act.astype(BF16), wd_ref[...], preferred_element_type=F32)


def _ffn(x3, g2, wu, wg, cw, cb, wd, *, tm):
    bsz, seq, _ = x3.shape
    kern = functools.partial(_ffn_kernel, tm=tm)
    return pl.pallas_call(
        kern,
        grid=(bsz, seq // tm),
        in_specs=[
            pl.BlockSpec((1, tm, D_MODEL), lambda b, s: (b, s, 0)),
            _const_spec((1, D_MODEL)),
            _const_spec((D_MODEL, D_FF)),
            _const_spec((D_MODEL, D_FF)),
            _const_spec((FFN_CONV_W, D_FF)),
            _const_spec((1, D_FF)),
            _const_spec((D_FF, D_MODEL)),
        ],
        out_specs=pl.BlockSpec((1, tm, D_MODEL), lambda b, s: (b, s, 0)),
        out_shape=jax.ShapeDtypeStruct((bsz, seq, D_MODEL), F32),
        scratch_shapes=[pltpu.VMEM((tm + SUBLANES, D_FF), F32)],
        compiler_params=pltpu.CompilerParams(
            dimension_semantics=("arbitrary", "arbitrary"), vmem_limit_bytes=VMEM_LIMIT),
        name="ffn",
    )(x3, g2, wu, wg, cw, cb, wd)


def _tile(seq, want):
    t = min(want, seq)
    assert seq % t == 0 and t % MOBA_BLOCK == 0
    return t


def kernel(x, norm1_g, w_in, conv_w, conv_b, w_r, b_r, w_i, b_i, lru_lambda, q_norm_g, k_norm_g,
           w_proj_rnn, w_proj_attn, w_out, norm2_g, w_up, w_gate, ffn_conv_w, ffn_conv_b, w_down):
    bsz, seq, d = x.shape
    assert d == D_MODEL and seq % MOBA_BLOCK == 0 and seq // MOBA_BLOCK <= LANES
    depth = norm1_g.shape[0]
    t = bsz * seq
    for l in range(depth):
        x2 = x.reshape(t, D_MODEL)
        wri = jnp.concatenate([w_r[l], w_i[l]], axis=-1).astype(BF16)
        bri = jnp.concatenate([b_r[l], b_i[l]], axis=-1)[:, None, :]
        ya, q, k, v, gates = _in_proj(
            x2, norm1_g[l][None], w_in[l].astype(BF16), q_norm_g[l][None], k_norm_g[l][None],
            conv_w[l], conv_b[l][None], wri, bri, lru_lambda[l][None], bsz=bsz, seq=seq)
        yb = _moba(q, k, v, hp=2)
        x1 = _merge(ya.reshape(t, D_RNN), yb.reshape(t, D_ATTN), gates, x2,
                    w_proj_rnn[l].astype(BF16), w_proj_attn[l].astype(BF16), w_out[l].astype(BF16),
                    tm=_tile(seq, 512))
        x = _ffn(x1.reshape(bsz, seq, D_MODEL), norm2_g[l][None], w_up[l].astype(BF16),
                 w_gate[l].astype(BF16), ffn_conv_w[l], ffn_conv_b[l][None], w_down[l].astype(BF16),
                 tm=_tile(seq, 256))
    return x
```

```python
import functools

import jax
import jax.numpy as jnp
from jax import lax
from jax.experimental import pallas as pl
from jax.experimental.pallas import tpu as pltpu

D_MODEL = 1024
D_RNN = 1024
RNN_BLOCKS = 8
RNN_BW = D_RNN // RNN_BLOCKS
CONV_W = 4
LRU_C = 8.0
N_HEADS = 8
HEAD_DIM = 128
D_ATTN = N_HEADS * HEAD_DIM
MOBA_BLOCK = 256
MOBA_TOPK = 3
D_FF = 2816
FFN_CONV_W = 3
EPS = 1e-6
NEG = -1e30
LOG2_E = 1.4426950408889634

LANES = 128
SUBLANES = 8
V_ROWS = HEAD_DIM + 2 * SUBLANES
MOBA_TILE = 2 * MOBA_BLOCK
LONG_BODY = 4
VMEM_LIMIT = 56 * 1024 * 1024

F32 = jnp.float32
BF16 = jnp.bfloat16

C_RNN = 0
C_Q = 2 * D_RNN
C_K = C_Q + D_ATTN
C_V = C_K + D_ATTN
C_G = C_V + D_ATTN
D_IN = C_G + 2 * D_MODEL


def _rms(y, g):
    return y * lax.rsqrt(jnp.mean(y * y, axis=-1, keepdims=True) + EPS) * g


def _gelu(x):
    k1 = -2.0 * 0.7978845608028654 * LOG2_E
    k2 = k1 * 0.044715
    return x * (1.0 / (1.0 + jnp.exp2(x * (k1 + k2 * (x * x)))))


def _const_spec(shape):
    nd = len(shape)
    return pl.BlockSpec(shape, lambda *_: (0,) * nd, pipeline_mode=pl.Buffered(1))


def _lru_scan(a, b, h0):
    rows, width = a.shape
    groups = rows // SUBLANES
    a = a.reshape(groups, SUBLANES, width)
    b = b.reshape(groups, SUBLANES, width)
    sub = lax.broadcasted_iota(jnp.int32, a.shape, 1)
    for d in (1, 2, 4):
        a_sh = jnp.where(sub >= d, pltpu.roll(a, d, 1), 1.0)
        b_sh = jnp.where(sub >= d, pltpu.roll(b, d, 1), 0.0)
        b = a * b_sh + b
        a = a * a_sh
    out = []
    for g in range(groups):
        hg = b[g] + a[g] * h0
        h0 = hg[SUBLANES - 1:SUBLANES]
        out.append(hg)
    return jnp.concatenate(out, axis=0), h0


def _in_proj_kernel(x_ref, g1_ref, w_ref, qg_ref, kg_ref, cw_ref, cb_ref, wri_ref, bri_ref, lam_ref,
                    ya_ref, q_ref, k_ref, v_ref, gate_ref, km_sc, xbuf, hst, hb_sc, y_sc, *, tiles_per_seq):
    tm = MOBA_BLOCK
    j = pl.program_id(0) % tiles_per_seq
    nt_dims = (((1,), (1,)), ((), ()))

    @pl.when(j == 0)
    def _():
        km_sc[...] = jnp.zeros_like(km_sc)
        xbuf[...] = jnp.zeros_like(xbuf)
        hst[...] = jnp.zeros_like(hst)

    hb_sc[...] = _rms(x_ref[...], g1_ref[...]).astype(BF16)
    pw = 4 * LANES

    y_sc[...] = jnp.dot(hb_sc[...], w_ref[...], preferred_element_type=F32)

    def proj(c0, width=pw):
        return y_sc[:, c0:c0 + width]

    xbuf[0:SUBLANES, :] = xbuf[tm:tm + SUBLANES, :]
    xbuf[SUBLANES:SUBLANES + tm, :] = proj(C_RNN, D_RNN)

    def rnn_block(n):
        cols = slice(n * RNN_BW, (n + 1) * RNN_BW)
        xa = cb_ref[:, cols]
        for k in range(CONV_W):
            off = SUBLANES - (CONV_W - 1) + k
            xa = xa + xbuf[off:off + tm, cols] * cw_ref[k:k + 1, cols]
        z = jnp.dot(xa.astype(BF16), wri_ref[n], preferred_element_type=F32) + bri_ref[n]
        r = jax.nn.sigmoid(z[:, 0:RNN_BW])
        gi = jax.nn.sigmoid(z[:, RNN_BW:2 * RNN_BW])
        a = jnp.exp(r * (-LRU_C * jax.nn.softplus(-lam_ref[:, cols])))
        w = 1.0 - a * a
        b = jnp.where(w > 0.0, w * lax.rsqrt(w), 0.0) * (gi * xa)
        hseq, hlast = _lru_scan(a, b, hst[0:1, cols])
        hst[0:1, cols] = hlast
        ya_ref[:, cols] = (hseq * _gelu(proj(C_RNN + D_RNN + n * RNN_BW, RNN_BW))).astype(BF16)

    nrow = km_sc.shape[0]
    blk = lax.broadcasted_iota(jnp.int32, (nrow, tm), 0)
    blk_f = blk.astype(F32)
    scale = HEAD_DIM ** -0.5 * LOG2_E

    def q_unit(p):
        yq = proj(C_Q + p * pw)
        for i in range(pw // HEAD_DIM):
            h = p * (pw // HEAD_DIM) + i
            cols = slice(h * HEAD_DIM, (h + 1) * HEAD_DIM)
            y = _rms(yq[:, i * HEAD_DIM:(i + 1) * HEAD_DIM], qg_ref[...]) * scale
            q_ref[0, h, :, 0:HEAD_DIM] = y.astype(BF16)
            gate_t = lax.dot_general(km_sc[:, cols], y, nt_dims,
                                     precision=lax.Precision.HIGHEST, preferred_element_type=F32)
            g = jnp.where(blk < j, gate_t, NEG)
            sel = jnp.zeros((nrow, tm), dtype=jnp.bool_)
            for slot in range(MOBA_TOPK):
                gmax = jnp.max(g, axis=0, keepdims=True)
                idx = jnp.min(jnp.where(g == gmax, blk_f, float(LANES)), axis=0, keepdims=True)
                hit = blk_f == idx
                sel = jnp.logical_or(sel, jnp.logical_and(hit, slot < j))
                g = jnp.where(hit, -3e38, g)
            negsel_t = jnp.where(jnp.logical_or(sel, blk == j), 0.0, NEG)
            if nrow < LANES:
                negsel_t = jnp.concatenate([negsel_t, jnp.zeros((LANES - nrow, tm), F32)], axis=0)
            q_ref[0, h, :, HEAD_DIM:2 * HEAD_DIM] = jnp.transpose(negsel_t).astype(BF16)

    lane = lax.broadcasted_iota(jnp.int32, (tm, LANES), 1)
    onehot = jnp.where(lane == j, 1.0, 0.0).astype(BF16)

    def k_unit(p):
        yk = proj(C_K + p * pw)
        for i in range(pw // HEAD_DIM):
            h = p * (pw // HEAD_DIM) + i
            cols = slice(h * HEAD_DIM, (h + 1) * HEAD_DIM)
            y = _rms(yk[:, i * HEAD_DIM:(i + 1) * HEAD_DIM], kg_ref[...])
            k_ref[0, h, :, 0:HEAD_DIM] = y.astype(BF16)
            k_ref[0, h, :, HEAD_DIM:2 * HEAD_DIM] = onehot
            km_sc[:, cols] = jnp.where(blk[:, 0:HEAD_DIM] == j, jnp.mean(y, axis=0, keepdims=True),
                                       km_sc[:, cols])

    ones = jnp.ones((V_ROWS - HEAD_DIM, tm), BF16)

    def v_unit(p):
        yv = proj(C_V + p * pw)
        for i in range(pw // HEAD_DIM):
            h = p * (pw // HEAD_DIM) + i
            v_ref[0, h, 0:HEAD_DIM, :] = jnp.transpose(yv[:, i * HEAD_DIM:(i + 1) * HEAD_DIM]).astype(BF16)
            v_ref[0, h, HEAD_DIM:V_ROWS, :] = ones

    def gate_unit(c):
        gate_ref[:, c * pw:(c + 1) * pw] = jax.nn.sigmoid(proj(C_G + c * pw)).astype(BF16)

    units = ([functools.partial(q_unit, p) for p in range(D_ATTN // pw)]
             + [functools.partial(k_unit, p) for p in range(D_ATTN // pw)]
             + [functools.partial(v_unit, p) for p in range(D_ATTN // pw)]
             + [functools.partial(gate_unit, c) for c in range(2 * D_MODEL // pw)])
    for n in range(RNN_BLOCKS):
        rnn_block(n)
        if n < len(units):
            units[n]()
    for unit in units[RNN_BLOCKS:]:
        unit()


def _in_proj(x2, g1, w_in, qg, kg, cw, cb, wri, bri, lam, *, bsz, seq):
    t = bsz * seq
    tm = MOBA_BLOCK
    nst = seq // tm
    nrow = -(-nst // SUBLANES) * SUBLANES
    kern = functools.partial(_in_proj_kernel, tiles_per_seq=nst)
    return pl.pallas_call(
        kern,
        grid=(t // tm,),
        in_specs=[
            pl.BlockSpec((tm, D_MODEL), lambda i: (i, 0)),
            _const_spec((1, D_MODEL)),
            _const_spec((D_MODEL, D_IN)),
            _const_spec((1, HEAD_DIM)),
            _const_spec((1, HEAD_DIM)),
            _const_spec((CONV_W, D_RNN)),
            _const_spec((1, D_RNN)),
            _const_spec((RNN_BLOCKS, RNN_BW, 2 * RNN_BW)),
            _const_spec((RNN_BLOCKS, 1, 2 * RNN_BW)),
            _const_spec((1, D_RNN)),
        ],
        out_specs=[
            pl.BlockSpec((tm, D_RNN), lambda i: (i, 0)),
            pl.BlockSpec((1, N_HEADS, tm, 2 * HEAD_DIM), lambda i: (i // nst, 0, i % nst, 0)),
            pl.BlockSpec((1, N_HEADS, tm, 2 * HEAD_DIM), lambda i: (i // nst, 0, i % nst, 0)),
            pl.BlockSpec((1, N_HEADS, V_ROWS, tm), lambda i: (i // nst, 0, 0, i % nst)),
            pl.BlockSpec((tm, 2 * D_MODEL), lambda i: (i, 0)),
        ],
        out_shape=[
            jax.ShapeDtypeStruct((t, D_RNN), BF16),
            jax.ShapeDtypeStruct((bsz, N_HEADS, seq, 2 * HEAD_DIM), BF16),
            jax.ShapeDtypeStruct((bsz, N_HEADS, seq, 2 * HEAD_DIM), BF16),
            jax.ShapeDtypeStruct((bsz, N_HEADS, V_ROWS, seq), BF16),
            jax.ShapeDtypeStruct((t, 2 * D_MODEL), BF16),
        ],
        scratch_shapes=[
            pltpu.VMEM((nrow, D_ATTN), F32),
            pltpu.VMEM((tm + SUBLANES, D_RNN), F32),
            pltpu.VMEM((SUBLANES, D_RNN), F32),
            pltpu.VMEM((tm, D_MODEL), BF16),
            pltpu.VMEM((tm, D_IN), F32),
        ],
        compiler_params=pltpu.CompilerParams(
            dimension_semantics=("arbitrary",), vmem_limit_bytes=VMEM_LIMIT),
        name="in_proj",
    )(x2, g1, w_in, qg, kg, cw, cb, wri, bri, lam)


def _moba_kernel(q_ref, k_ref, v_ref, o_ref, sa_sc, sb_sc, xa_sc, xb_sc, m_sc, acc_sc, *, hp):
    t = pl.program_id(2)
    kc = MOBA_TILE
    nt_dims = (((1,), (1,)), ((), ()))
    heads = range(hp)

    def scores(h, chunk, dst):
        st = pl.multiple_of(chunk * kc, kc)
        sc = lax.dot_general(k_ref[0, h, pl.ds(st, kc), :], q_ref[0, h], nt_dims,
                             preferred_element_type=F32)
        dst[0][h] = sc
        dst[1][h] = jnp.max(sc, axis=0, keepdims=True)

    def accumulate(h, chunk, src, causal=False):
        st = pl.multiple_of(chunk * kc, kc)
        sc = src[0][h]
        if causal:
            key_i = lax.broadcasted_iota(jnp.int32, sc.shape, 0)
            qry_i = lax.broadcasted_iota(jnp.int32, sc.shape, 1)
            sc = jnp.where(key_i <= qry_i, sc, NEG)
        m_prev = m_sc[h]
        m_new = jnp.maximum(m_prev, src[1][h])
        p = jnp.exp2(sc - m_new).astype(BF16)
        acc_sc[h] = jnp.exp2(m_prev - m_new) * acc_sc[h] + jnp.dot(
            v_ref[0, h, :, pl.ds(st, kc)], p, preferred_element_type=F32)
        m_sc[h] = m_new

    m_sc[...] = jnp.full_like(m_sc, NEG)
    acc_sc[...] = jnp.zeros_like(acc_sc)
    buf_a, buf_b = (sa_sc, xa_sc), (sb_sc, xb_sc)
    for h in heads:
        scores(h, 0, buf_a)

    def step(c, cur, nxt):
        for h in heads:
            scores(h, c + 1, nxt)
        for h in heads:
            accumulate(h, c, cur)

    def run(first, trips, unroll):
        def body(i, carry):
            for u in range(unroll):
                cur, nxt = (buf_a, buf_b) if u % 2 == 0 else (buf_b, buf_a)
                step(first + unroll * i + u, cur, nxt)
            return carry

        lax.fori_loop(0, trips, body, 0)

    n_long = lax.div(t, LONG_BODY)
    rest = t - n_long * LONG_BODY
    run(0, n_long, LONG_BODY)
    run(n_long * LONG_BODY, lax.div(rest, 2), 2)

    @pl.when(rest % 2 == 1)
    def _():
        step(t - 1, buf_a, buf_b)
        for h in heads:
            accumulate(h, t, buf_b, causal=True)

    @pl.when(rest % 2 == 0)
    def _():
        for h in heads:
            accumulate(h, t, buf_a, causal=True)

    for h in heads:
        acc = acc_sc[h]
        o_t = acc[0:HEAD_DIM, :] / acc[HEAD_DIM:HEAD_DIM + 1, :]
        o_ref[0, :, h * HEAD_DIM:(h + 1) * HEAD_DIM] = jnp.transpose(o_t).astype(BF16)


def _moba(q, k, v, *, hp):
    bsz, nh, seq, _ = q.shape
    assert seq % MOBA_TILE == 0 and nh % hp == 0
    kc = MOBA_TILE
    kern = functools.partial(_moba_kernel, hp=hp)
    return pl.pallas_call(
        kern,
        grid=(bsz, nh // hp, seq // MOBA_TILE),
        in_specs=[
            pl.BlockSpec((1, hp, MOBA_TILE, 2 * HEAD_DIM), lambda b, h, j: (b, h, j, 0)),
            pl.BlockSpec((1, hp, seq, 2 * HEAD_DIM), lambda b, h, j: (b, h, 0, 0)),
            pl.BlockSpec((1, hp, V_ROWS, seq), lambda b, h, j: (b, h, 0, 0)),
        ],
        out_specs=pl.BlockSpec((1, MOBA_TILE, hp * HEAD_DIM), lambda b, h, j: (b, j, h)),
        out_shape=jax.ShapeDtypeStruct((bsz, seq, D_ATTN), BF16),
        scratch_shapes=[
            pltpu.VMEM((hp, kc, MOBA_TILE), F32),
            pltpu.VMEM((hp, kc, MOBA_TILE), F32),
            pltpu.VMEM((hp, 1, MOBA_TILE), F32),
            pltpu.VMEM((hp, 1, MOBA_TILE), F32),
            pltpu.VMEM((hp, 1, MOBA_TILE), F32),
            pltpu.VMEM((hp, V_ROWS, MOBA_TILE), F32),
        ],
        compiler_params=pltpu.CompilerParams(
            dimension_semantics=("arbitrary", "arbitrary", "arbitrary"), vmem_limit_bytes=VMEM_LIMIT),
        name="moba",
    )(q, k, v)


def _merge_kernel(ya_ref, yb_ref, gate_ref, x_ref, pa_ref, pb_ref, wo_ref, o_ref):
    a = jnp.dot(ya_ref[...], pa_ref[...], preferred_element_type=F32)
    b = jnp.dot(yb_ref[...], pb_ref[...], preferred_element_type=F32)
    merged = (gate_ref[:, 0:D_MODEL].astype(F32) * a
              + gate_ref[:, D_MODEL:2 * D_MODEL].astype(F32) * b)
    o_ref[...] = x_ref[...] + jnp.dot(merged.astype(BF16), wo_ref[...], preferred_element_type=F32)


def _merge(ya, yb, gates, x2, pa, pb, wo, *, tm):
    t = x2.shape[0]
    row = lambda w: pl.BlockSpec((tm, w), lambda i: (i, 0))
    return pl.pallas_call(
        _merge_kernel,
        grid=(t // tm,),
        in_specs=[row(D_RNN), row(D_ATTN), row(2 * D_MODEL), row(D_MODEL),
                  _const_spec((D_RNN, D_MODEL)), _const_spec((D_ATTN, D_MODEL)),
                  _const_spec((D_MODEL, D_MODEL))],
        out_specs=row(D_MODEL),
        out_shape=jax.ShapeDtypeStruct((t, D_MODEL), F32),
        compiler_params=pltpu.CompilerParams(
            dimension_semantics=("arbitrary",), vmem_limit_bytes=VMEM_LIMIT),
        name="merge",
    )(ya, yb, gates, x2, pa, pb, wo)


def _ffn_kernel(x_ref, g2_ref, wu_ref, wg_ref, cw_ref, cb_ref, wd_ref, o_ref, ubuf, *, tm):
    s = pl.program_id(1)

    @pl.when(s == 0)
    def _():
        ubuf[...] = jnp.zeros_like(ubuf)

    x = x_ref[0]
    hb = _rms(x, g2_ref[...]).astype(BF16)
    ubuf[0:SUBLANES, :] = ubuf[tm:tm + SUBLANES, :]
    ubuf[SUBLANES:SUBLANES + tm, :] = jnp.dot(hb, wu_ref[...], preferred_element_type=F32)
    conv = cb_ref[...]
    for k in range(FFN_CONV_W):
        off = SUBLANES - (FFN_CONV_W - 1) + k
        conv = conv + ubuf[off:off + tm, :] * cw_ref[k:k + 1, :]
    act = _gelu(conv) * jnp.dot(hb, wg_ref[...], preferred_element_type=F32)
    o_ref[0] = x + jnp.dot(act.astype(BF16), wd_ref[...], preferred_element_type=F32)


def _ffn(x3, g2, wu, wg, cw, cb, wd, *, tm):
    bsz, seq, _ = x3.shape
    kern = functools.partial(_ffn_kernel, tm=tm)
    return pl.pallas_call(
        kern,
        grid=(bsz, seq // tm),
        in_specs=[
            pl.BlockSpec((1, tm, D_MODEL), lambda b, s: (b, s, 0)),
            _const_spec((1, D_MODEL)),
            _const_spec((D_MODEL, D_FF)),
            _const_spec((D_MODEL, D_FF)),
            _const_spec((FFN_CONV_W, D_FF)),
            _const_spec((1, D_FF)),
            _const_spec((D_FF, D_MODEL)),
        ],
        out_specs=pl.BlockSpec((1, tm, D_MODEL), lambda b, s: (b, s, 0)),
        out_shape=jax.ShapeDtypeStruct((bsz, seq, D_MODEL), F32),
        scratch_shapes=[pltpu.VMEM((tm + SUBLANES, D_FF), F32)],
        compiler_params=pltpu.CompilerParams(
            dimension_semantics=("arbitrary", "arbitrary"), vmem_limit_bytes=VMEM_LIMIT),
        name="ffn",
    )(x3, g2, wu, wg, cw, cb, wd)


def _tile(seq, want):
    t = min(want, seq)
    assert seq % t == 0 and t % MOBA_BLOCK == 0
    return t


def kernel(x, norm1_g, w_in, conv_w, conv_b, w_r, b_r, w_i, b_i, lru_lambda, q_norm_g, k_norm_g,
           w_proj_rnn, w_proj_attn, w_out, norm2_g, w_up, w_gate, ffn_conv_w, ffn_conv_b, w_down):
    bsz, seq, d = x.shape
    assert d == D_MODEL and seq % MOBA_BLOCK == 0 and seq // MOBA_BLOCK <= LANES
    depth = norm1_g.shape[0]
    t = bsz * seq
    for l in range(depth):
        x2 = x.reshape(t, D_MODEL)
        wri = jnp.concatenate([w_r[l], w_i[l]], axis=-1).astype(BF16)
        bri = jnp.concatenate([b_r[l], b_i[l]], axis=-1)[:, None, :]
        ya, q, k, v, gates = _in_proj(
            x2, norm1_g[l][None], w_in[l].astype(BF16), q_norm_g[l][None], k_norm_g[l][None],
            conv_w[l], conv_b[l][None], wri, bri, lru_lambda[l][None], bsz=bsz, seq=seq)
        yb = _moba(q, k, v, hp=2)
        x1 = _merge(ya.reshape(t, D_RNN), yb.reshape(t, D_ATTN), gates, x2,
                    w_proj_rnn[l].astype(BF16), w_proj_attn[l].astype(BF16), w_out[l].astype(BF16),
                    tm=_tile(seq, 512))
        x = _ffn(x1.reshape(bsz, seq, D_MODEL), norm2_g[l][None], w_up[l].astype(BF16),
                 w_gate[l].astype(BF16), ffn_conv_w[l], ffn_conv_b[l][None], w_down[l].astype(BF16),
                 tm=_tile(seq, 512))
    return x
```

```python
import functools

import jax
import jax.numpy as jnp
from jax import lax
from jax.experimental import pallas as pl
from jax.experimental.pallas import tpu as pltpu

D_MODEL = 1024
D_RNN = 1024
RNN_BLOCKS = 8
RNN_BW = D_RNN // RNN_BLOCKS
CONV_W = 4
LRU_C = 8.0
N_HEADS = 8
HEAD_DIM = 128
D_ATTN = N_HEADS * HEAD_DIM
MOBA_BLOCK = 256
MOBA_TOPK = 3
D_FF = 2816
FFN_CONV_W = 3
EPS = 1e-6
NEG = -1e30
LOG2_E = 1.4426950408889634

LANES = 128
SUBLANES = 8
V_ROWS = HEAD_DIM + 2 * SUBLANES
MOBA_TILE = 2 * MOBA_BLOCK
LOOP_TIERS = (8, 4, 2)
VMEM_LIMIT = 56 * 1024 * 1024

F32 = jnp.float32
BF16 = jnp.bfloat16

C_RNN = 0
C_Q = 2 * D_RNN
C_K = C_Q + D_ATTN
C_V = C_K + D_ATTN
C_G = C_V + D_ATTN
D_IN = C_G + 2 * D_MODEL


def _rms(y, g):
    return y * lax.rsqrt(jnp.mean(y * y, axis=-1, keepdims=True) + EPS) * g


def _gelu(x):
    k1 = -2.0 * 0.7978845608028654 * LOG2_E
    k2 = k1 * 0.044715
    return x * (1.0 / (1.0 + jnp.exp2(x * (k1 + k2 * (x * x)))))


def _const_spec(shape):
    nd = len(shape)
    return pl.BlockSpec(shape, lambda *_: (0,) * nd, pipeline_mode=pl.Buffered(1))


def _lru_scan(a, b, h0):
    rows, width = a.shape
    groups = rows // SUBLANES
    a = a.reshape(groups, SUBLANES, width)
    b = b.reshape(groups, SUBLANES, width)
    sub = lax.broadcasted_iota(jnp.int32, a.shape, 1)
    for d in (1, 2, 4):
        a_sh = jnp.where(sub >= d, pltpu.roll(a, d, 1), 1.0)
        b_sh = jnp.where(sub >= d, pltpu.roll(b, d, 1), 0.0)
        b = a * b_sh + b
        a = a * a_sh
    out = []
    for g in range(groups):
        hg = b[g] + a[g] * h0
        h0 = hg[SUBLANES - 1:SUBLANES]
        out.append(hg)
    return jnp.concatenate(out, axis=0), h0


def _in_proj_kernel(x_ref, g1_ref, w_ref, qg_ref, kg_ref, cw_ref, cb_ref, wri_ref, bri_ref, lam_ref,
                    ya_ref, q_ref, k_ref, v_ref, gate_ref, km_sc, xbuf, hst, hb_sc, y_sc, *, tiles_per_seq):
    tm = MOBA_BLOCK
    j = pl.program_id(0) % tiles_per_seq
    nt_dims = (((1,), (1,)), ((), ()))

    @pl.when(j == 0)
    def _():
        km_sc[...] = jnp.zeros_like(km_sc)
        xbuf[...] = jnp.zeros_like(xbuf)
        hst[...] = jnp.zeros_like(hst)

    hb_sc[...] = _rms(x_ref[...], g1_ref[...]).astype(BF16)
    pw = 4 * LANES

    y_sc[...] = jnp.dot(hb_sc[...], w_ref[...], preferred_element_type=F32)

    def proj(c0, width=pw):
        return y_sc[:, c0:c0 + width]

    xbuf[0:SUBLANES, :] = xbuf[tm:tm + SUBLANES, :]
    xbuf[SUBLANES:SUBLANES + tm, :] = proj(C_RNN, D_RNN)

    def rnn_block(n):
        cols = slice(n * RNN_BW, (n + 1) * RNN_BW)
        xa = cb_ref[:, cols]
        for k in range(CONV_W):
            off = SUBLANES - (CONV_W - 1) + k
            xa = xa + xbuf[off:off + tm, cols] * cw_ref[k:k + 1, cols]
        z = jnp.dot(xa.astype(BF16), wri_ref[n], preferred_element_type=F32) + bri_ref[n]
        r = jax.nn.sigmoid(z[:, 0:RNN_BW])
        gi = jax.nn.sigmoid(z[:, RNN_BW:2 * RNN_BW])
        a = jnp.exp(r * (-LRU_C * jax.nn.softplus(-lam_ref[:, cols])))
        w = 1.0 - a * a
        b = jnp.where(w > 0.0, w * lax.rsqrt(w), 0.0) * (gi * xa)
        hseq, hlast = _lru_scan(a, b, hst[0:1, cols])
        hst[0:1, cols] = hlast
        ya_ref[:, cols] = (hseq * _gelu(proj(C_RNN + D_RNN + n * RNN_BW, RNN_BW))).astype(BF16)

    nrow = km_sc.shape[0]
    blk = lax.broadcasted_iota(jnp.int32, (nrow, tm), 0)
    blk_f = blk.astype(F32)
    scale = HEAD_DIM ** -0.5 * LOG2_E

    def q_unit(p):
        yq = proj(C_Q + p * pw)
        for i in range(pw // HEAD_DIM):
            h = p * (pw // HEAD_DIM) + i
            cols = slice(h * HEAD_DIM, (h + 1) * HEAD_DIM)
            y = _rms(yq[:, i * HEAD_DIM:(i + 1) * HEAD_DIM], qg_ref[...]) * scale
            q_ref[0, h, 0:HEAD_DIM, :] = jnp.transpose(y).astype(BF16)
            gate_t = lax.dot_general(km_sc[:, cols], y, nt_dims,
                                     precision=lax.Precision.HIGHEST, preferred_element_type=F32)
            g = jnp.where(blk < j, gate_t, NEG)
            sel = jnp.zeros((nrow, tm), dtype=jnp.bool_)
            for slot in range(MOBA_TOPK):
                gmax = jnp.max(g, axis=0, keepdims=True)
                idx = jnp.min(jnp.where(g == gmax, blk_f, float(LANES)), axis=0, keepdims=True)
                hit = blk_f == idx
                sel = jnp.logical_or(sel, jnp.logical_and(hit, slot < j))
                g = jnp.where(hit, -3e38, g)
            negsel_t = jnp.where(jnp.logical_or(sel, blk == j), 0.0, NEG)
            if nrow < LANES:
                negsel_t = jnp.concatenate([negsel_t, jnp.zeros((LANES - nrow, tm), F32)], axis=0)
            q_ref[0, h, HEAD_DIM:2 * HEAD_DIM, :] = negsel_t.astype(BF16)

    lane = lax.broadcasted_iota(jnp.int32, (tm, LANES), 1)
    onehot = jnp.where(lane == j, 1.0, 0.0).astype(BF16)

    def k_unit(p):
        yk = proj(C_K + p * pw)
        for i in range(pw // HEAD_DIM):
            h = p * (pw // HEAD_DIM) + i
            cols = slice(h * HEAD_DIM, (h + 1) * HEAD_DIM)
            y = _rms(yk[:, i * HEAD_DIM:(i + 1) * HEAD_DIM], kg_ref[...])
            k_ref[0, h, :, 0:HEAD_DIM] = y.astype(BF16)
            k_ref[0, h, :, HEAD_DIM:2 * HEAD_DIM] = onehot
            km_sc[:, cols] = jnp.where(blk[:, 0:HEAD_DIM] == j, jnp.mean(y, axis=0, keepdims=True),
                                       km_sc[:, cols])

    ones = jnp.ones((V_ROWS - HEAD_DIM, tm), BF16)

    def v_unit(p):
        yv = proj(C_V + p * pw)
        for i in range(pw // HEAD_DIM):
            h = p * (pw // HEAD_DIM) + i
            v_ref[0, h, 0:HEAD_DIM, :] = jnp.transpose(yv[:, i * HEAD_DIM:(i + 1) * HEAD_DIM]).astype(BF16)
            v_ref[0, h, HEAD_DIM:V_ROWS, :] = ones

    def gate_unit(c):
        gate_ref[:, c * pw:(c + 1) * pw] = jax.nn.sigmoid(proj(C_G + c * pw)).astype(BF16)

    units = ([functools.partial(q_unit, p) for p in range(D_ATTN // pw)]
             + [functools.partial(k_unit, p) for p in range(D_ATTN // pw)]
             + [functools.partial(v_unit, p) for p in range(D_ATTN // pw)]
             + [functools.partial(gate_unit, c) for c in range(2 * D_MODEL // pw)])
    for n in range(RNN_BLOCKS):
        rnn_block(n)
        if n < len(units):
            units[n]()
    for unit in units[RNN_BLOCKS:]:
        unit()


def _in_proj(x2, g1, w_in, qg, kg, cw, cb, wri, bri, lam, *, bsz, seq):
    t = bsz * seq
    tm = MOBA_BLOCK
    nst = seq // tm
    nrow = -(-nst // SUBLANES) * SUBLANES
    kern = functools.partial(_in_proj_kernel, tiles_per_seq=nst)
    return pl.pallas_call(
        kern,
        grid=(t // tm,),
        in_specs=[
            pl.BlockSpec((tm, D_MODEL), lambda i: (i, 0)),
            _const_spec((1, D_MODEL)),
            _const_spec((D_MODEL, D_IN)),
            _const_spec((1, HEAD_DIM)),
            _const_spec((1, HEAD_DIM)),
            _const_spec((CONV_W, D_RNN)),
            _const_spec((1, D_RNN)),
            _const_spec((RNN_BLOCKS, RNN_BW, 2 * RNN_BW)),
            _const_spec((RNN_BLOCKS, 1, 2 * RNN_BW)),
            _const_spec((1, D_RNN)),
        ],
        out_specs=[
            pl.BlockSpec((tm, D_RNN), lambda i: (i, 0)),
            pl.BlockSpec((1, N_HEADS, 2 * HEAD_DIM, tm), lambda i: (i // nst, 0, 0, i % nst)),
            pl.BlockSpec((1, N_HEADS, tm, 2 * HEAD_DIM), lambda i: (i // nst, 0, i % nst, 0)),
            pl.BlockSpec((1, N_HEADS, V_ROWS, tm), lambda i: (i // nst, 0, 0, i % nst)),
            pl.BlockSpec((tm, 2 * D_MODEL), lambda i: (i, 0)),
        ],
        out_shape=[
            jax.ShapeDtypeStruct((t, D_RNN), BF16),
            jax.ShapeDtypeStruct((bsz, N_HEADS, 2 * HEAD_DIM, seq), BF16),
            jax.ShapeDtypeStruct((bsz, N_HEADS, seq, 2 * HEAD_DIM), BF16),
            jax.ShapeDtypeStruct((bsz, N_HEADS, V_ROWS, seq), BF16),
            jax.ShapeDtypeStruct((t, 2 * D_MODEL), BF16),
        ],
        scratch_shapes=[
            pltpu.VMEM((nrow, D_ATTN), F32),
            pltpu.VMEM((tm + SUBLANES, D_RNN), F32),
            pltpu.VMEM((SUBLANES, D_RNN), F32),
            pltpu.VMEM((tm, D_MODEL), BF16),
            pltpu.VMEM((tm, D_IN), F32),
        ],
        compiler_params=pltpu.CompilerParams(
            dimension_semantics=("arbitrary",), vmem_limit_bytes=VMEM_LIMIT),
        name="in_proj",
    )(x2, g1, w_in, qg, kg, cw, cb, wri, bri, lam)


def _moba_kernel(q_ref, k_ref, v_ref, o_ref, sa_sc, sb_sc, xa_sc, xb_sc, m_sc, acc_sc, *, hp):
    t = pl.program_id(2)
    kc = MOBA_TILE
    nt_dims = (((1,), (1,)), ((), ()))
    heads = range(hp)

    def scores(h, chunk, dst):
        st = pl.multiple_of(chunk * kc, kc)
        sc = jnp.dot(k_ref[0, h, pl.ds(st, kc), :], q_ref[0, h], preferred_element_type=F32)
        dst[0][h] = sc
        dst[1][h] = jnp.max(sc, axis=0, keepdims=True)

    def accumulate(h, chunk, src, causal=False):
        st = pl.multiple_of(chunk * kc, kc)
        sc = src[0][h]
        if causal:
            key_i = lax.broadcasted_iota(jnp.int32, sc.shape, 0)
            qry_i = lax.broadcasted_iota(jnp.int32, sc.shape, 1)
            sc = jnp.where(key_i <= qry_i, sc, NEG)
        m_prev = m_sc[h]
        m_new = jnp.maximum(m_prev, src[1][h])
        p = jnp.exp2(sc - m_new).astype(BF16)
        acc_sc[h] = jnp.exp2(m_prev - m_new) * acc_sc[h] + jnp.dot(
            v_ref[0, h, :, pl.ds(st, kc)], p, preferred_element_type=F32)
        m_sc[h] = m_new

    m_sc[...] = jnp.full_like(m_sc, NEG)
    acc_sc[...] = jnp.zeros_like(acc_sc)
    buf_a, buf_b = (sa_sc, xa_sc), (sb_sc, xb_sc)
    for h in heads:
        scores(h, 0, buf_a)

    def step(c, cur, nxt):
        for h in heads:
            scores(h, c + 1, nxt)
        for h in heads:
            accumulate(h, c, cur)

    def run(first, trips, unroll):
        def body(i, carry):
            for u in range(unroll):
                cur, nxt = (buf_a, buf_b) if u % 2 == 0 else (buf_b, buf_a)
                step(first + unroll * i + u, cur, nxt)
            return carry

        lax.fori_loop(0, trips, body, 0)

    done, rest = 0, t
    for unroll in LOOP_TIERS:
        trips = lax.div(rest, unroll)
        run(done, trips, unroll)
        done = done + trips * unroll
        rest = rest - trips * unroll

    @pl.when(rest % 2 == 1)
    def _():
        step(t - 1, buf_a, buf_b)
        for h in heads:
            accumulate(h, t, buf_b, causal=True)

    @pl.when(rest % 2 == 0)
    def _():
        for h in heads:
            accumulate(h, t, buf_a, causal=True)

    for h in heads:
        acc = acc_sc[h]
        o_t = acc[0:HEAD_DIM, :] / acc[HEAD_DIM:HEAD_DIM + 1, :]
        o_ref[0, :, h * HEAD_DIM:(h + 1) * HEAD_DIM] = jnp.transpose(o_t).astype(BF16)


def _moba(q, k, v, *, hp):
    bsz, nh, _, seq = q.shape
    assert seq % MOBA_TILE == 0 and nh % hp == 0
    kc = MOBA_TILE
    kern = functools.partial(_moba_kernel, hp=hp)
    return pl.pallas_call(
        kern,
        grid=(bsz, nh // hp, seq // MOBA_TILE),
        in_specs=[
            pl.BlockSpec((1, hp, 2 * HEAD_DIM, MOBA_TILE), lambda b, h, j: (b, h, 0, j)),
            pl.BlockSpec((1, hp, seq, 2 * HEAD_DIM), lambda b, h, j: (b, h, 0, 0)),
            pl.BlockSpec((1, hp, V_ROWS, seq), lambda b, h, j: (b, h, 0, 0)),
        ],
        out_specs=pl.BlockSpec((1, MOBA_TILE, hp * HEAD_DIM), lambda b, h, j: (b, j, h)),
        out_shape=jax.ShapeDtypeStruct((bsz, seq, D_ATTN), BF16),
        scratch_shapes=[
            pltpu.VMEM((hp, kc, MOBA_TILE), F32),
            pltpu.VMEM((hp, kc, MOBA_TILE), F32),
            pltpu.VMEM((hp, 1, MOBA_TILE), F32),
            pltpu.VMEM((hp, 1, MOBA_TILE), F32),
            pltpu.VMEM((hp, 1, MOBA_TILE), F32),
            pltpu.VMEM((hp, V_ROWS, MOBA_TILE), F32),
        ],
        compiler_params=pltpu.CompilerParams(
            dimension_semantics=("arbitrary", "arbitrary", "arbitrary"), vmem_limit_bytes=VMEM_LIMIT),
        name="moba",
    )(q, k, v)


def _merge_kernel(ya_ref, yb_ref, gate_ref, x_ref, pa_ref, pb_ref, wo_ref, o_ref):
    a = jnp.dot(ya_ref[...], pa_ref[...], preferred_element_type=F32)
    b = jnp.dot(yb_ref[...], pb_ref[...], preferred_element_type=F32)
    merged = (gate_ref[:, 0:D_MODEL].astype(F32) * a
              + gate_ref[:, D_MODEL:2 * D_MODEL].astype(F32) * b)
    o_ref[...] = x_ref[...] + jnp.dot(merged.astype(BF16), wo_ref[...], preferred_element_type=F32)


def _merge(ya, yb, gates, x2, pa, pb, wo, *, tm):
    t = x2.shape[0]
    row = lambda w: pl.BlockSpec((tm, w), lambda i: (i, 0))
    return pl.pallas_call(
        _merge_kernel,
        grid=(t // tm,),
        in_specs=[row(D_RNN), row(D_ATTN), row(2 * D_MODEL), row(D_MODEL),
                  _const_spec((D_RNN, D_MODEL)), _const_spec((D_ATTN, D_MODEL)),
                  _const_spec((D_MODEL, D_MODEL))],
        out_specs=row(D_MODEL),
        out_shape=jax.ShapeDtypeStruct((t, D_MODEL), F32),
        compiler_params=pltpu.CompilerParams(
            dimension_semantics=("arbitrary",), vmem_limit_bytes=VMEM_LIMIT),
        name="merge",
    )(ya, yb, gates, x2, pa, pb, wo)


def _ffn_kernel(x_ref, g2_ref, wu_ref, wg_ref, cw_ref, cb_ref, wd_ref, o_ref, ubuf, *, tm):
    s = pl.program_id(1)

    @pl.when(s == 0)
    def _():
        ubuf[...] = jnp.zeros_like(ubuf)

    x = x_ref[0]
    hb = _rms(x, g2_ref[...]).astype(BF16)
    ubuf[0:SUBLANES, :] = ubuf[tm:tm + SUBLANES, :]
    ubuf[SUBLANES:SUBLANES + tm, :] = jnp.dot(hb, wu_ref[...], preferred_element_type=F32)
    conv = cb_ref[...]
    for k in range(FFN_CONV_W):
        off = SUBLANES - (FFN_CONV_W - 1) + k
        conv = conv + ubuf[off:off + tm, :] * cw_ref[k:k + 1, :]
    act = _gelu(conv) * jnp.dot(hb, wg_ref[...], preferred_element_type=F32)
    o_ref[0] = x + jnp.dot(act.astype(BF16), wd_ref[...], preferred_element_type=F32)


def _ffn(x3, g2, wu, wg, cw, cb, wd, *, tm):
    bsz, seq, _ = x3.shape
    kern = functools.partial(_ffn_kernel, tm=tm)
    return pl.pallas_call(
        kern,
        grid=(bsz, seq // tm),
        in_specs=[
            pl.BlockSpec((1, tm, D_MODEL), lambda b, s: (b, s, 0)),
            _const_spec((1, D_MODEL)),
            _const_spec((D_MODEL, D_FF)),
            _const_spec((D_MODEL, D_FF)),
            _const_spec((FFN_CONV_W, D_FF)),
            _const_spec((1, D_FF)),
            _const_spec((D_FF, D_MODEL)),
        ],
        out_specs=pl.BlockSpec((1, tm, D_MODEL), lambda b, s: (b, s, 0)),
        out_shape=jax.ShapeDtypeStruct((bsz, seq, D_MODEL), F32),
        scratch_shapes=[pltpu.VMEM((tm + SUBLANES, D_FF), F32)],
        compiler_params=pltpu.CompilerParams(
            dimension_semantics=("arbitrary", "arbitrary"), vmem_limit_bytes=VMEM_LIMIT),
        name="ffn",
    )(x3, g2, wu, wg, cw, cb, wd)


def _tile(seq, want):
    t = min(want, seq)
    assert seq % t == 0 and t % MOBA_BLOCK == 0
    return t


def kernel(x, norm1_g, w_in, conv_w, conv_b, w_r, b_r, w_i, b_i, lru_lambda, q_norm_g, k_norm_g,
           w_proj_rnn, w_proj_attn, w_out, norm2_g, w_up, w_gate, ffn_conv_w, ffn_conv_b, w_down):
    bsz, seq, d = x.shape
    assert d == D_MODEL and seq % MOBA_BLOCK == 0 and seq // MOBA_BLOCK <= LANES
    depth = norm1_g.shape[0]
    t = bsz * seq
    for l in range(depth):
        x2 = x.reshape(t, D_MODEL)
        wri = jnp.concatenate([w_r[l], w_i[l]], axis=-1).astype(BF16)
        bri = jnp.concatenate([b_r[l], b_i[l]], axis=-1)[:, None, :]
        ya, q, k, v, gates = _in_proj(
            x2, norm1_g[l][None], w_in[l].astype(BF16), q_norm_g[l][None], k_norm_g[l][None],
            conv_w[l], conv_b[l][None], wri, bri, lru_lambda[l][None], bsz=bsz, seq=seq)
        yb = _moba(q, k, v, hp=2)
        x1 = _merge(ya.reshape(t, D_RNN), yb.reshape(t, D_ATTN), gates, x2,
                    w_proj_rnn[l].astype(BF16), w_proj_attn[l].astype(BF16), w_out[l].astype(BF16),
                    tm=_tile(seq, 512))
        x = _ffn(x1.reshape(bsz, seq, D_MODEL), norm2_g[l][None], w_up[l].astype(BF16),
                 w_gate[l].astype(BF16), ffn_conv_w[l], ffn_conv_b[l][None], w_down[l].astype(BF16),
                 tm=_tile(seq, 512))
    return x
```

```python
import functools

import jax
import jax.numpy as jnp
from jax import lax
from jax.experimental import pallas as pl
from jax.experimental.pallas import tpu as pltpu

D_MODEL = 1024
D_RNN = 1024
RNN_BLOCKS = 8
RNN_BW = D_RNN // RNN_BLOCKS
CONV_W = 4
LRU_C = 8.0
N_HEADS = 8
HEAD_DIM = 128
D_ATTN = N_HEADS * HEAD_DIM
MOBA_BLOCK = 256
MOBA_TOPK = 3
D_FF = 2816
FFN_CONV_W = 3
EPS = 1e-6
NEG = -1e30
TAKEN = -3e38
LOG2_E = 1.4426950408889634

LANES = 128
SUBLANES = 8
V_ROWS = HEAD_DIM + 2 * SUBLANES
MOBA_TILE = 2 * MOBA_BLOCK
LOOP_TIERS = (8, 4, 2)
VMEM_LIMIT = 56 * 1024 * 1024

F32 = jnp.float32
BF16 = jnp.bfloat16

C_RNN = 0
C_Q = 2 * D_RNN
C_K = C_Q + D_ATTN
C_V = C_K + D_ATTN
C_G = C_V + D_ATTN
D_IN = C_G + 2 * D_MODEL


def _rms(y, g):
    return y * lax.rsqrt(jnp.mean(y * y, axis=-1, keepdims=True) + EPS) * g


def _gelu(x):
    k1 = -2.0 * 0.7978845608028654 * LOG2_E
    k2 = k1 * 0.044715
    return x * (1.0 / (1.0 + jnp.exp2(x * (k1 + k2 * (x * x)))))


def _const_spec(shape):
    nd = len(shape)
    return pl.BlockSpec(shape, lambda *_: (0,) * nd, pipeline_mode=pl.Buffered(1))


def _lru_scan(a, b, h0):
    rows, width = a.shape
    groups = rows // SUBLANES
    a = a.reshape(groups, SUBLANES, width)
    b = b.reshape(groups, SUBLANES, width)
    sub = lax.broadcasted_iota(jnp.int32, a.shape, 1)
    for d in (1, 2, 4):
        a_sh = jnp.where(sub >= d, pltpu.roll(a, d, 1), 1.0)
        b_sh = jnp.where(sub >= d, pltpu.roll(b, d, 1), 0.0)
        b = a * b_sh + b
        a = a * a_sh
    out = []
    for g in range(groups):
        hg = b[g] + a[g] * h0
        h0 = hg[SUBLANES - 1:SUBLANES]
        out.append(hg)
    return jnp.concatenate(out, axis=0), h0


def _in_proj_kernel(x_ref, g1_ref, w_ref, qg_ref, kg_ref, cw_ref, cb_ref, wri_ref, bri_ref, lam_ref,
                    ya_ref, q_ref, k_ref, v_ref, gate_ref, km_sc, xbuf, hst, hb_sc, y_sc, *, tiles_per_seq):
    tm = MOBA_BLOCK
    j = pl.program_id(0) % tiles_per_seq
    nt_dims = (((1,), (1,)), ((), ()))

    @pl.when(j == 0)
    def _():
        km_sc[...] = jnp.zeros_like(km_sc)
        xbuf[...] = jnp.zeros_like(xbuf)
        hst[...] = jnp.zeros_like(hst)

    hb_sc[...] = _rms(x_ref[...], g1_ref[...]).astype(BF16)
    pw = 4 * LANES

    y_sc[...] = jnp.dot(hb_sc[...], w_ref[...], preferred_element_type=F32)

    def proj(c0, width=pw):
        return y_sc[:, c0:c0 + width]

    xbuf[0:SUBLANES, :] = xbuf[tm:tm + SUBLANES, :]
    xbuf[SUBLANES:SUBLANES + tm, :] = proj(C_RNN, D_RNN)

    def rnn_block(n):
        cols = slice(n * RNN_BW, (n + 1) * RNN_BW)
        xa = cb_ref[:, cols]
        for k in range(CONV_W):
            off = SUBLANES - (CONV_W - 1) + k
            xa = xa + xbuf[off:off + tm, cols] * cw_ref[k:k + 1, cols]
        z = jnp.dot(xa.astype(BF16), wri_ref[n], preferred_element_type=F32) + bri_ref[n]
        r = jax.nn.sigmoid(z[:, 0:RNN_BW])
        gi = jax.nn.sigmoid(z[:, RNN_BW:2 * RNN_BW])
        a = jnp.exp(r * (-LRU_C * jax.nn.softplus(-lam_ref[:, cols])))
        w = 1.0 - a * a
        b = jnp.where(w > 0.0, w * lax.rsqrt(w), 0.0) * (gi * xa)
        hseq, hlast = _lru_scan(a, b, hst[0:1, cols])
        hst[0:1, cols] = hlast
        ya_ref[:, cols] = hseq.astype(BF16)
        ya_ref[:, D_RNN + n * RNN_BW:D_RNN + (n + 1) * RNN_BW] = proj(
            C_RNN + D_RNN + n * RNN_BW, RNN_BW).astype(BF16)

    nrow = km_sc.shape[0]
    blk = lax.broadcasted_iota(jnp.int32, (nrow, tm), 0)
    blk_f = blk.astype(F32)
    scale = HEAD_DIM ** -0.5 * LOG2_E

    def q_unit(p):
        yq = proj(C_Q + p * pw)
        for i in range(pw // HEAD_DIM):
            h = p * (pw // HEAD_DIM) + i
            cols = slice(h * HEAD_DIM, (h + 1) * HEAD_DIM)
            y = _rms(yq[:, i * HEAD_DIM:(i + 1) * HEAD_DIM], qg_ref[...]) * scale
            q_ref[0, h, 0:HEAD_DIM, :] = jnp.transpose(y).astype(BF16)
            gate_t = lax.dot_general(km_sc[:, cols], y, nt_dims,
                                     precision=lax.Precision.HIGHEST, preferred_element_type=F32)
            g = jnp.where(blk < j, gate_t, jnp.where(blk == j, TAKEN, NEG))
            for slot in range(MOBA_TOPK):
                gmax = jnp.max(g, axis=0, keepdims=True)
                idx = jnp.min(jnp.where(g == gmax, blk_f, float(LANES)), axis=0, keepdims=True)
                g = jnp.where(jnp.logical_and(blk_f == idx, slot < j), TAKEN, g)
            negsel_t = jnp.where(g == TAKEN, 0.0, NEG)
            if nrow < LANES:
                negsel_t = jnp.concatenate([negsel_t, jnp.zeros((LANES - nrow, tm), F32)], axis=0)
            q_ref[0, h, HEAD_DIM:2 * HEAD_DIM, :] = negsel_t.astype(BF16)

    lane = lax.broadcasted_iota(jnp.int32, (tm, LANES), 1)
    onehot = jnp.where(lane == j, 1.0, 0.0).astype(BF16)

    def k_unit(p):
        yk = proj(C_K + p * pw)
        for i in range(pw // HEAD_DIM):
            h = p * (pw // HEAD_DIM) + i
            cols = slice(h * HEAD_DIM, (h + 1) * HEAD_DIM)
            y = _rms(yk[:, i * HEAD_DIM:(i + 1) * HEAD_DIM], kg_ref[...])
            k_ref[0, h, :, 0:HEAD_DIM] = y.astype(BF16)
            k_ref[0, h, :, HEAD_DIM:2 * HEAD_DIM] = onehot
            km_sc[:, cols] = jnp.where(blk[:, 0:HEAD_DIM] == j, jnp.mean(y, axis=0, keepdims=True),
                                       km_sc[:, cols])

    ones = jnp.ones((V_ROWS - HEAD_DIM, tm), BF16)

    def v_unit(p):
        yv = proj(C_V + p * pw)
        for i in range(pw // HEAD_DIM):
            h = p * (pw // HEAD_DIM) + i
            v_ref[0, h, 0:HEAD_DIM, :] = jnp.transpose(yv[:, i * HEAD_DIM:(i + 1) * HEAD_DIM]).astype(BF16)
            v_ref[0, h, HEAD_DIM:V_ROWS, :] = ones

    def gate_unit(c):
        gate_ref[:, c * pw:(c + 1) * pw] = proj(C_G + c * pw).astype(BF16)

    units = ([functools.partial(q_unit, p) for p in range(D_ATTN // pw)]
             + [functools.partial(k_unit, p) for p in range(D_ATTN // pw)]
             + [functools.partial(v_unit, p) for p in range(D_ATTN // pw)]
             + [functools.partial(gate_unit, c) for c in range(2 * D_MODEL // pw)])
    for n in range(RNN_BLOCKS):
        rnn_block(n)
        if n < len(units):
            units[n]()
    for unit in units[RNN_BLOCKS:]:
        unit()


def _in_proj(x2, g1, w_in, qg, kg, cw, cb, wri, bri, lam, *, bsz, seq):
    t = bsz * seq
    tm = MOBA_BLOCK
    nst = seq // tm
    nrow = -(-nst // SUBLANES) * SUBLANES
    kern = functools.partial(_in_proj_kernel, tiles_per_seq=nst)
    return pl.pallas_call(
        kern,
        grid=(t // tm,),
        in_specs=[
            pl.BlockSpec((tm, D_MODEL), lambda i: (i, 0)),
            _const_spec((1, D_MODEL)),
            _const_spec((D_MODEL, D_IN)),
            _const_spec((1, HEAD_DIM)),
            _const_spec((1, HEAD_DIM)),
            _const_spec((CONV_W, D_RNN)),
            _const_spec((1, D_RNN)),
            _const_spec((RNN_BLOCKS, RNN_BW, 2 * RNN_BW)),
            _const_spec((RNN_BLOCKS, 1, 2 * RNN_BW)),
            _const_spec((1, D_RNN)),
        ],
        out_specs=[
            pl.BlockSpec((tm, 2 * D_RNN), lambda i: (i, 0)),
            pl.BlockSpec((1, N_HEADS, 2 * HEAD_DIM, tm), lambda i: (i // nst, 0, 0, i % nst)),
            pl.BlockSpec((1, N_HEADS, tm, 2 * HEAD_DIM), lambda i: (i // nst, 0, i % nst, 0)),
            pl.BlockSpec((1, N_HEADS, V_ROWS, tm), lambda i: (i // nst, 0, 0, i % nst)),
            pl.BlockSpec((tm, 2 * D_MODEL), lambda i: (i, 0)),
        ],
        out_shape=[
            jax.ShapeDtypeStruct((t, 2 * D_RNN), BF16),
            jax.ShapeDtypeStruct((bsz, N_HEADS, 2 * HEAD_DIM, seq), BF16),
            jax.ShapeDtypeStruct((bsz, N_HEADS, seq, 2 * HEAD_DIM), BF16),
            jax.ShapeDtypeStruct((bsz, N_HEADS, V_ROWS, seq), BF16),
            jax.ShapeDtypeStruct((t, 2 * D_MODEL), BF16),
        ],
        scratch_shapes=[
            pltpu.VMEM((nrow, D_ATTN), F32),
            pltpu.VMEM((tm + SUBLANES, D_RNN), F32),
            pltpu.VMEM((SUBLANES, D_RNN), F32),
            pltpu.VMEM((tm, D_MODEL), BF16),
            pltpu.VMEM((tm, D_IN), F32),
        ],
        compiler_params=pltpu.CompilerParams(
            dimension_semantics=("arbitrary",), vmem_limit_bytes=VMEM_LIMIT),
        name="in_proj",
    )(x2, g1, w_in, qg, kg, cw, cb, wri, bri, lam)


def _moba_kernel(q_ref, k_ref, v_ref, o_ref, sa_sc, sb_sc, xa_sc, xb_sc, m_sc, acc_sc, *, hp):
    t = pl.program_id(2)
    kc = MOBA_TILE
    heads = range(hp)

    def scores(h, chunk, dst):
        st = pl.multiple_of(chunk * kc, kc)
        sc = jnp.dot(k_ref[0, h, pl.ds(st, kc), :], q_ref[0, h], preferred_element_type=F32)
        dst[0][h] = sc
        dst[1][h] = jnp.max(sc, axis=0, keepdims=True)

    def accumulate(h, chunk, src, causal=False):
        st = pl.multiple_of(chunk * kc, kc)
        sc = src[0][h]
        if causal:
            key_i = lax.broadcasted_iota(jnp.int32, sc.shape, 0)
            qry_i = lax.broadcasted_iota(jnp.int32, sc.shape, 1)
            sc = jnp.where(key_i <= qry_i, sc, NEG)
        m_prev = m_sc[h]
        m_new = jnp.maximum(m_prev, src[1][h])
        p = jnp.exp2(sc - m_new).astype(BF16)
        acc_sc[h] = jnp.exp2(m_prev - m_new) * acc_sc[h] + jnp.dot(
            v_ref[0, h, :, pl.ds(st, kc)], p, preferred_element_type=F32)
        m_sc[h] = m_new

    m_sc[...] = jnp.full_like(m_sc, NEG)
    acc_sc[...] = jnp.zeros_like(acc_sc)
    buf_a, buf_b = (sa_sc, xa_sc), (sb_sc, xb_sc)
    for h in heads:
        scores(h, 0, buf_a)

    def step(c, cur, nxt):
        for h in heads:
            scores(h, c + 1, nxt)
        for h in heads:
            accumulate(h, c, cur)

    def run(first, trips, unroll):
        def body(i, carry):
            for u in range(unroll):
                cur, nxt = (buf_a, buf_b) if u % 2 == 0 else (buf_b, buf_a)
                step(first + unroll * i + u, cur, nxt)
            return carry

        lax.fori_loop(0, trips, body, 0)

    done, rest = 0, t
    for unroll in LOOP_TIERS:
        trips = lax.div(rest, unroll)
        run(done, trips, unroll)
        done = done + trips * unroll
        rest = rest - trips * unroll

    @pl.when(rest % 2 == 1)
    def _():
        step(t - 1, buf_a, buf_b)
        for h in heads:
            accumulate(h, t, buf_b, causal=True)

    @pl.when(rest % 2 == 0)
    def _():
        for h in heads:
            accumulate(h, t, buf_a, causal=True)

    for h in heads:
        acc = acc_sc[h]
        o_t = acc[0:HEAD_DIM, :] / acc[HEAD_DIM:HEAD_DIM + 1, :]
        o_ref[0, :, h * HEAD_DIM:(h + 1) * HEAD_DIM] = jnp.transpose(o_t).astype(BF16)


def _moba(q, k, v, *, hp):
    bsz, nh, _, seq = q.shape
    assert seq % MOBA_TILE == 0 and nh % hp == 0
    kc = MOBA_TILE
    kern = functools.partial(_moba_kernel, hp=hp)
    return pl.pallas_call(
        kern,
        grid=(bsz, nh // hp, seq // MOBA_TILE),
        in_specs=[
            pl.BlockSpec((1, hp, 2 * HEAD_DIM, MOBA_TILE), lambda b, h, j: (b, h, 0, j)),
            pl.BlockSpec((1, hp, seq, 2 * HEAD_DIM), lambda b, h, j: (b, h, 0, 0)),
            pl.BlockSpec((1, hp, V_ROWS, seq), lambda b, h, j: (b, h, 0, 0)),
        ],
        out_specs=pl.BlockSpec((1, MOBA_TILE, hp * HEAD_DIM), lambda b, h, j: (b, j, h)),
        out_shape=jax.ShapeDtypeStruct((bsz, seq, D_ATTN), BF16),
        scratch_shapes=[
            pltpu.VMEM((hp, kc, MOBA_TILE), F32),
            pltpu.VMEM((hp, kc, MOBA_TILE), F32),
            pltpu.VMEM((hp, 1, MOBA_TILE), F32),
            pltpu.VMEM((hp, 1, MOBA_TILE), F32),
            pltpu.VMEM((hp, 1, MOBA_TILE), F32),
            pltpu.VMEM((hp, V_ROWS, MOBA_TILE), F32),
        ],
        compiler_params=pltpu.CompilerParams(
            dimension_semantics=("arbitrary", "arbitrary", "arbitrary"), vmem_limit_bytes=VMEM_LIMIT),
        name="moba",
    )(q, k, v)


def _merge_kernel(ya_ref, yb_ref, gate_ref, x_ref, pa_ref, pb_ref, wo_ref, o_ref):
    ya = (ya_ref[:, 0:D_RNN].astype(F32) * _gelu(ya_ref[:, D_RNN:2 * D_RNN].astype(F32))).astype(BF16)
    a = jnp.dot(ya, pa_ref[...], preferred_element_type=F32)
    b = jnp.dot(yb_ref[...], pb_ref[...], preferred_element_type=F32)
    merged = (jax.nn.sigmoid(gate_ref[:, 0:D_MODEL].astype(F32)) * a
              + jax.nn.sigmoid(gate_ref[:, D_MODEL:2 * D_MODEL].astype(F32)) * b)
    o_ref[...] = x_ref[...] + jnp.dot(merged.astype(BF16), wo_ref[...], preferred_element_type=F32)


def _merge(ya, yb, gates, x2, pa, pb, wo, *, tm):
    t = x2.shape[0]
    row = lambda w: pl.BlockSpec((tm, w), lambda i: (i, 0))
    return pl.pallas_call(
        _merge_kernel,
        grid=(t // tm,),
        in_specs=[row(2 * D_RNN), row(D_ATTN), row(2 * D_MODEL), row(D_MODEL),
                  _const_spec((D_RNN, D_MODEL)), _const_spec((D_ATTN, D_MODEL)),
                  _const_spec((D_MODEL, D_MODEL))],
        out_specs=row(D_MODEL),
        out_shape=jax.ShapeDtypeStruct((t, D_MODEL), F32),
        compiler_params=pltpu.CompilerParams(
            dimension_semantics=("arbitrary",), vmem_limit_bytes=VMEM_LIMIT),
        name="merge",
    )(ya, yb, gates, x2, pa, pb, wo)


def _ffn_kernel(x_ref, g2_ref, wu_ref, wg_ref, cw_ref, cb_ref, wd_ref, o_ref, ubuf, *, tm):
    s = pl.program_id(1)

    @pl.when(s == 0)
    def _():
        ubuf[...] = jnp.zeros_like(ubuf)

    x = x_ref[0]
    hb = _rms(x, g2_ref[...]).astype(BF16)
    ubuf[0:SUBLANES, :] = ubuf[tm:tm + SUBLANES, :]
    ubuf[SUBLANES:SUBLANES + tm, :] = jnp.dot(hb, wu_ref[...], preferred_element_type=F32)
    conv = cb_ref[...]
    for k in range(FFN_CONV_W):
        off = SUBLANES - (FFN_CONV_W - 1) + k
        conv = conv + ubuf[off:off + tm, :] * cw_ref[k:k + 1, :]
    act = _gelu(conv) * jnp.dot(hb, wg_ref[...], preferred_element_type=F32)
    o_ref[0] = x + jnp.dot(act.astype(BF16), wd_ref[...], preferred_element_type=F32)


def _ffn(x3, g2, wu, wg, cw, cb, wd, *, tm):
    bsz, seq, _ = x3.shape
    kern = functools.partial(_ffn_kernel, tm=tm)
    return pl.pallas_call(
        kern,
        grid=(bsz, seq // tm),
        in_specs=[
            pl.BlockSpec((1, tm, D_MODEL), lambda b, s: (b, s, 0)),
            _const_spec((1, D_MODEL)),
            _const_spec((D_MODEL, D_FF)),
            _const_spec((D_MODEL, D_FF)),
            _const_spec((FFN_CONV_W, D_FF)),
            _const_spec((1, D_FF)),
            _const_spec((D_FF, D_MODEL)),
        ],
        out_specs=pl.BlockSpec((1, tm, D_MODEL), lambda b, s: (b, s, 0)),
        out_shape=jax.ShapeDtypeStruct((bsz, seq, D_MODEL), F32),
        scratch_shapes=[pltpu.VMEM((tm + SUBLANES, D_FF), F32)],
        compiler_params=pltpu.CompilerParams(
            dimension_semantics=("arbitrary", "arbitrary"), vmem_limit_bytes=VMEM_LIMIT),
        name="ffn",
    )(x3, g2, wu, wg, cw, cb, wd)


def _tile(seq, want):
    t = min(want, seq)
    assert seq % t == 0 and t % MOBA_BLOCK == 0
    return t


def kernel(x, norm1_g, w_in, conv_w, conv_b, w_r, b_r, w_i, b_i, lru_lambda, q_norm_g, k_norm_g,
           w_proj_rnn, w_proj_attn, w_out, norm2_g, w_up, w_gate, ffn_conv_w, ffn_conv_b, w_down):
    bsz, seq, d = x.shape
    assert d == D_MODEL and seq % MOBA_BLOCK == 0 and seq // MOBA_BLOCK <= LANES
    depth = norm1_g.shape[0]
    t = bsz * seq
    for l in range(depth):
        x2 = x.reshape(t, D_MODEL)
        wri = jnp.concatenate([w_r[l], w_i[l]], axis=-1).astype(BF16)
        bri = jnp.concatenate([b_r[l], b_i[l]], axis=-1)[:, None, :]
        ya, q, k, v, gates = _in_proj(
            x2, norm1_g[l][None], w_in[l].astype(BF16), q_norm_g[l][None], k_norm_g[l][None],
            conv_w[l], conv_b[l][None], wri, bri, lru_lambda[l][None], bsz=bsz, seq=seq)
        yb = _moba(q, k, v, hp=1)
        x1 = _merge(ya, yb.reshape(t, D_ATTN), gates, x2,
                    w_proj_rnn[l].astype(BF16), w_proj_attn[l].astype(BF16), w_out[l].astype(BF16),
                    tm=_tile(seq, 512))
        x = _ffn(x1.reshape(bsz, seq, D_MODEL), norm2_g[l][None], w_up[l].astype(BF16),
                 w_gate[l].astype(BF16), ffn_conv_w[l], ffn_conv_b[l][None], w_down[l].astype(BF16),
                 tm=_tile(seq, 512))
    return x
```

```python
import functools

import jax
import jax.numpy as jnp
from jax import lax
from jax.experimental import pallas as pl
from jax.experimental.pallas import tpu as pltpu

D_MODEL = 1024
D_RNN = 1024
RNN_BLOCKS = 8
RNN_BW = D_RNN // RNN_BLOCKS
CONV_W = 4
LRU_C = 8.0
N_HEADS = 8
HEAD_DIM = 128
D_ATTN = N_HEADS * HEAD_DIM
MOBA_BLOCK = 256
MOBA_TOPK = 3
D_FF = 2816
FFN_CONV_W = 3
EPS = 1e-6
NEG = -1e30
TAKEN = -3e38
LOG2_E = 1.4426950408889634

LANES = 128
SUBLANES = 8
V_ROWS = HEAD_DIM + 2 * SUBLANES
MOBA_TILE = 2 * MOBA_BLOCK
LOOP_TIERS = (8, 4, 2)
VMEM_LIMIT = 56 * 1024 * 1024

F32 = jnp.float32
BF16 = jnp.bfloat16

C_RNN = 0
C_Q = 2 * D_RNN
C_K = C_Q + D_ATTN
C_V = C_K + D_ATTN
C_G = C_V + D_ATTN
D_IN = C_G + 2 * D_MODEL


def _rms(y, g):
    return y * lax.rsqrt(jnp.mean(y * y, axis=-1, keepdims=True) + EPS) * g


def _gelu(x):
    k1 = -2.0 * 0.7978845608028654 * LOG2_E
    k2 = k1 * 0.044715
    return x * (1.0 / (1.0 + jnp.exp2(x * (k1 + k2 * (x * x)))))


def _const_spec(shape):
    nd = len(shape)
    return pl.BlockSpec(shape, lambda *_: (0,) * nd, pipeline_mode=pl.Buffered(1))


def _lru_scan(a, b, h0):
    rows, width = a.shape
    groups = rows // SUBLANES
    a = a.reshape(groups, SUBLANES, width)
    b = b.reshape(groups, SUBLANES, width)
    sub = lax.broadcasted_iota(jnp.int32, a.shape, 1)
    for d in (1, 2, 4):
        a_sh = jnp.where(sub >= d, pltpu.roll(a, d, 1), 1.0)
        b_sh = jnp.where(sub >= d, pltpu.roll(b, d, 1), 0.0)
        b = a * b_sh + b
        a = a * a_sh
    out = []
    for g in range(groups):
        hg = b[g] + a[g] * h0
        h0 = hg[SUBLANES - 1:SUBLANES]
        out.append(hg)
    return jnp.concatenate(out, axis=0), h0


def _in_proj_kernel(x_ref, g1_ref, w_ref, qg_ref, kg_ref, cw_ref, cb_ref, wri_ref, bri_ref, lam_ref,
                    ya_ref, q_ref, k_ref, v_ref, gate_ref, km_sc, xbuf, hst, hb_sc, y_sc, *, tiles_per_seq):
    tm = MOBA_BLOCK
    j = pl.program_id(0) % tiles_per_seq
    nt_dims = (((1,), (1,)), ((), ()))

    @pl.when(j == 0)
    def _():
        km_sc[...] = jnp.zeros_like(km_sc)
        xbuf[...] = jnp.zeros_like(xbuf)
        hst[...] = jnp.zeros_like(hst)

    hb_sc[...] = _rms(x_ref[...], g1_ref[...]).astype(BF16)
    pw = 4 * LANES

    y_sc[...] = jnp.dot(hb_sc[...], w_ref[...], preferred_element_type=F32)

    def proj(c0, width=pw):
        return y_sc[:, c0:c0 + width]

    xbuf[0:SUBLANES, :] = xbuf[tm:tm + SUBLANES, :]
    xbuf[SUBLANES:SUBLANES + tm, :] = proj(C_RNN, D_RNN)

    def rnn_block(n):
        cols = slice(n * RNN_BW, (n + 1) * RNN_BW)
        xa = cb_ref[:, cols]
        for k in range(CONV_W):
            off = SUBLANES - (CONV_W - 1) + k
            xa = xa + xbuf[off:off + tm, cols] * cw_ref[k:k + 1, cols]
        z = jnp.dot(xa.astype(BF16), wri_ref[n], preferred_element_type=F32) + bri_ref[n]
        r = jax.nn.sigmoid(z[:, 0:RNN_BW])
        gi = jax.nn.sigmoid(z[:, RNN_BW:2 * RNN_BW])
        a = jnp.exp(r * (-LRU_C * jax.nn.softplus(-lam_ref[:, cols])))
        w = 1.0 - a * a
        b = jnp.where(w > 0.0, w * lax.rsqrt(w), 0.0) * (gi * xa)
        hseq, hlast = _lru_scan(a, b, hst[0:1, cols])
        hst[0:1, cols] = hlast
        ya_ref[:, cols] = hseq.astype(BF16)
        ya_ref[:, D_RNN + n * RNN_BW:D_RNN + (n + 1) * RNN_BW] = proj(
            C_RNN + D_RNN + n * RNN_BW, RNN_BW).astype(BF16)

    nrow = km_sc.shape[0]
    blk = lax.broadcasted_iota(jnp.int32, (nrow, tm), 0)
    blk_f = blk.astype(F32)
    scale = HEAD_DIM ** -0.5 * LOG2_E

    def q_unit(p):
        yq = proj(C_Q + p * pw)
        for i in range(pw // HEAD_DIM):
            h = p * (pw // HEAD_DIM) + i
            cols = slice(h * HEAD_DIM, (h + 1) * HEAD_DIM)
            y = _rms(yq[:, i * HEAD_DIM:(i + 1) * HEAD_DIM], qg_ref[...]) * scale
            q_ref[0, h, 0:HEAD_DIM, :] = jnp.transpose(y).astype(BF16)
            gate_t = lax.dot_general(km_sc[:, cols], y, nt_dims,
                                     precision=lax.Precision.HIGHEST, preferred_element_type=F32)
            g = jnp.where(blk < j, gate_t, jnp.where(blk == j, TAKEN, NEG))
            for slot in range(MOBA_TOPK):
                gmax = jnp.max(g, axis=0, keepdims=True)
                idx = jnp.min(jnp.where(g == gmax, blk_f, float(LANES)), axis=0, keepdims=True)
                g = jnp.where(jnp.logical_and(blk_f == idx, slot < j), TAKEN, g)
            negsel_t = jnp.where(g == TAKEN, 0.0, NEG)
            if nrow < LANES:
                negsel_t = jnp.concatenate([negsel_t, jnp.zeros((LANES - nrow, tm), F32)], axis=0)
            q_ref[0, h, HEAD_DIM:2 * HEAD_DIM, :] = negsel_t.astype(BF16)

    lane = lax.broadcasted_iota(jnp.int32, (tm, LANES), 1)
    onehot = jnp.where(lane == j, 1.0, 0.0).astype(BF16)

    def k_unit(p):
        yk = proj(C_K + p * pw)
        for i in range(pw // HEAD_DIM):
            h = p * (pw // HEAD_DIM) + i
            cols = slice(h * HEAD_DIM, (h + 1) * HEAD_DIM)
            y = _rms(yk[:, i * HEAD_DIM:(i + 1) * HEAD_DIM], kg_ref[...])
            k_ref[0, h, :, 0:HEAD_DIM] = y.astype(BF16)
            k_ref[0, h, :, HEAD_DIM:2 * HEAD_DIM] = onehot
            km_sc[:, cols] = jnp.where(blk[:, 0:HEAD_DIM] == j, jnp.mean(y, axis=0, keepdims=True),
                                       km_sc[:, cols])

    ones = jnp.ones((V_ROWS - HEAD_DIM, tm), BF16)

    def v_unit(p):
        yv = proj(C_V + p * pw)
        for i in range(pw // HEAD_DIM):
            h = p * (pw // HEAD_DIM) + i
            v_ref[0, h, 0:HEAD_DIM, :] = jnp.transpose(yv[:, i * HEAD_DIM:(i + 1) * HEAD_DIM]).astype(BF16)
            v_ref[0, h, HEAD_DIM:V_ROWS, :] = ones

    def gate_unit(c):
        gate_ref[:, c * pw:(c + 1) * pw] = proj(C_G + c * pw).astype(BF16)

    units = ([functools.partial(q_unit, p) for p in range(D_ATTN // pw)]
             + [functools.partial(k_unit, p) for p in range(D_ATTN // pw)]
             + [functools.partial(v_unit, p) for p in range(D_ATTN // pw)]
             + [functools.partial(gate_unit, c) for c in range(2 * D_MODEL // pw)])
    for n in range(RNN_BLOCKS):
        rnn_block(n)
        if n < len(units):
            units[n]()
    for unit in units[RNN_BLOCKS:]:
        unit()


def _in_proj(x2, g1, w_in, qg, kg, cw, cb, wri, bri, lam, *, bsz, seq):
    t = bsz * seq
    tm = MOBA_BLOCK
    nst = seq // tm
    nrow = -(-nst // SUBLANES) * SUBLANES
    kern = functools.partial(_in_proj_kernel, tiles_per_seq=nst)
    return pl.pallas_call(
        kern,
        grid=(t // tm,),
        in_specs=[
            pl.BlockSpec((tm, D_MODEL), lambda i: (i, 0)),
            _const_spec((1, D_MODEL)),
            _const_spec((D_MODEL, D_IN)),
            _const_spec((1, HEAD_DIM)),
            _const_spec((1, HEAD_DIM)),
            _const_spec((CONV_W, D_RNN)),
            _const_spec((1, D_RNN)),
            _const_spec((RNN_BLOCKS, RNN_BW, 2 * RNN_BW)),
            _const_spec((RNN_BLOCKS, 1, 2 * RNN_BW)),
            _const_spec((1, D_RNN)),
        ],
        out_specs=[
            pl.BlockSpec((tm, 2 * D_RNN), lambda i: (i, 0)),
            pl.BlockSpec((1, N_HEADS, 2 * HEAD_DIM, tm), lambda i: (i // nst, 0, 0, i % nst)),
            pl.BlockSpec((1, N_HEADS, tm, 2 * HEAD_DIM), lambda i: (i // nst, 0, i % nst, 0)),
            pl.BlockSpec((1, N_HEADS, V_ROWS, tm), lambda i: (i // nst, 0, 0, i % nst)),
            pl.BlockSpec((tm, 2 * D_MODEL), lambda i: (i, 0)),
        ],
        out_shape=[
            jax.ShapeDtypeStruct((t, 2 * D_RNN), BF16),
            jax.ShapeDtypeStruct((bsz, N_HEADS, 2 * HEAD_DIM, seq), BF16),
            jax.ShapeDtypeStruct((bsz, N_HEADS, seq, 2 * HEAD_DIM), BF16),
            jax.ShapeDtypeStruct((bsz, N_HEADS, V_ROWS, seq), BF16),
            jax.ShapeDtypeStruct((t, 2 * D_MODEL), BF16),
        ],
        scratch_shapes=[
            pltpu.VMEM((nrow, D_ATTN), F32),
            pltpu.VMEM((tm + SUBLANES, D_RNN), F32),
            pltpu.VMEM((SUBLANES, D_RNN), F32),
            pltpu.VMEM((tm, D_MODEL), BF16),
            pltpu.VMEM((tm, D_IN), F32),
        ],
        compiler_params=pltpu.CompilerParams(
            dimension_semantics=("arbitrary",), vmem_limit_bytes=VMEM_LIMIT),
        name="in_proj",
    )(x2, g1, w_in, qg, kg, cw, cb, wri, bri, lam)


def _moba_kernel(q_ref, qn_ref, k_ref, v_ref, o_ref, sa_sc, sb_sc, xa_sc, xb_sc, m_sc, acc_sc, *, hp):
    t = pl.program_id(2)
    kc = MOBA_TILE
    heads = range(hp)

    def scores(h, chunk, dst, q=q_ref):
        st = pl.multiple_of(chunk * kc, kc)
        sc = jnp.dot(k_ref[0, h, pl.ds(st, kc), :], q[0, h], preferred_element_type=F32)
        dst[0][h] = sc
        dst[1][h] = jnp.max(sc, axis=0, keepdims=True)

    def accumulate(h, chunk, src, causal=False):
        st = pl.multiple_of(chunk * kc, kc)
        sc = src[0][h]
        if causal:
            key_i = lax.broadcasted_iota(jnp.int32, sc.shape, 0)
            qry_i = lax.broadcasted_iota(jnp.int32, sc.shape, 1)
            sc = jnp.where(key_i <= qry_i, sc, NEG)
        m_prev = m_sc[h]
        m_new = jnp.maximum(m_prev, src[1][h])
        p = jnp.exp2(sc - m_new).astype(BF16)
        acc_sc[h] = jnp.exp2(m_prev - m_new) * acc_sc[h] + jnp.dot(
            v_ref[0, h, :, pl.ds(st, kc)], p, preferred_element_type=F32)
        m_sc[h] = m_new

    m_sc[...] = jnp.full_like(m_sc, NEG)
    acc_sc[...] = jnp.zeros_like(acc_sc)
    buf_a, buf_b = (sa_sc, xa_sc), (sb_sc, xb_sc)

    @pl.when(t == 0)
    def _():
        for h in heads:
            scores(h, 0, buf_a)

    def step(c, cur, nxt):
        for h in heads:
            scores(h, c + 1, nxt)
        for h in heads:
            accumulate(h, c, cur)

    def run(first, trips, unroll):
        def body(i, carry):
            for u in range(unroll):
                cur, nxt = (buf_a, buf_b) if u % 2 == 0 else (buf_b, buf_a)
                step(first + unroll * i + u, cur, nxt)
            return carry

        lax.fori_loop(0, trips, body, 0)

    done, rest = 0, t
    for unroll in LOOP_TIERS:
        trips = lax.div(rest, unroll)
        run(done, trips, unroll)
        done = done + trips * unroll
        rest = rest - trips * unroll

    @pl.when(rest % 2 == 1)
    def _():
        step(t - 1, buf_a, buf_b)
        for h in heads:
            accumulate(h, t, buf_b, causal=True)

    @pl.when(rest % 2 == 0)
    def _():
        for h in heads:
            accumulate(h, t, buf_a, causal=True)

    for h in heads:
        acc = acc_sc[h]
        o_t = acc[0:HEAD_DIM, :] / acc[HEAD_DIM:HEAD_DIM + 1, :]
        o_ref[0, :, h * HEAD_DIM:(h + 1) * HEAD_DIM] = jnp.transpose(o_t).astype(BF16)
        scores(h, 0, buf_a, qn_ref)


def _moba(q, k, v, *, hp):
    bsz, nh, _, seq = q.shape
    assert seq % MOBA_TILE == 0 and nh % hp == 0
    kc = MOBA_TILE
    last = seq // MOBA_TILE - 1
    kern = functools.partial(_moba_kernel, hp=hp)
    return pl.pallas_call(
        kern,
        grid=(bsz, nh // hp, seq // MOBA_TILE),
        in_specs=[
            pl.BlockSpec((1, hp, 2 * HEAD_DIM, MOBA_TILE), lambda b, h, j: (b, h, 0, j)),
            pl.BlockSpec((1, hp, 2 * HEAD_DIM, MOBA_TILE), lambda b, h, j: (b, h, 0, jnp.minimum(j + 1, last))),
            pl.BlockSpec((1, hp, seq, 2 * HEAD_DIM), lambda b, h, j: (b, h, 0, 0)),
            pl.BlockSpec((1, hp, V_ROWS, seq), lambda b, h, j: (b, h, 0, 0)),
        ],
        out_specs=pl.BlockSpec((1, MOBA_TILE, hp * HEAD_DIM), lambda b, h, j: (b, j, h)),
        out_shape=jax.ShapeDtypeStruct((bsz, seq, D_ATTN), BF16),
        scratch_shapes=[
            pltpu.VMEM((hp, kc, MOBA_TILE), F32),
            pltpu.VMEM((hp, kc, MOBA_TILE), F32),
            pltpu.VMEM((hp, 1, MOBA_TILE), F32),
            pltpu.VMEM((hp, 1, MOBA_TILE), F32),
            pltpu.VMEM((hp, 1, MOBA_TILE), F32),
            pltpu.VMEM((hp, V_ROWS, MOBA_TILE), F32),
        ],
        compiler_params=pltpu.CompilerParams(
            dimension_semantics=("arbitrary", "arbitrary", "arbitrary"), vmem_limit_bytes=VMEM_LIMIT),
        name="moba",
    )(q, q, k, v)


def _merge_kernel(ya_ref, yb_ref, gate_ref, x_ref, pa_ref, pb_ref, wo_ref, o_ref):
    ya = (ya_ref[:, 0:D_RNN].astype(F32) * _gelu(ya_ref[:, D_RNN:2 * D_RNN].astype(F32))).astype(BF16)
    a = jnp.dot(ya, pa_ref[...], preferred_element_type=F32)
    b = jnp.dot(yb_ref[...], pb_ref[...], preferred_element_type=F32)
    merged = (jax.nn.sigmoid(gate_ref[:, 0:D_MODEL].astype(F32)) * a
              + jax.nn.sigmoid(gate_ref[:, D_MODEL:2 * D_MODEL].astype(F32)) * b)
    o_ref[...] = x_ref[...] + jnp.dot(merged.astype(BF16), wo_ref[...], preferred_element_type=F32)


def _merge(ya, yb, gates, x2, pa, pb, wo, *, tm):
    t = x2.shape[0]
    row = lambda w: pl.BlockSpec((tm, w), lambda i: (i, 0))
    return pl.pallas_call(
        _merge_kernel,
        grid=(t // tm,),
        in_specs=[row(2 * D_RNN), row(D_ATTN), row(2 * D_MODEL), row(D_MODEL),
                  _const_spec((D_RNN, D_MODEL)), _const_spec((D_ATTN, D_MODEL)),
                  _const_spec((D_MODEL, D_MODEL))],
        out_specs=row(D_MODEL),
        out_shape=jax.ShapeDtypeStruct((t, D_MODEL), F32),
        compiler_params=pltpu.CompilerParams(
            dimension_semantics=("arbitrary",), vmem_limit_bytes=VMEM_LIMIT),
        name="merge",
    )(ya, yb, gates, x2, pa, pb, wo)


def _ffn_kernel(x_ref, g2_ref, wu_ref, wg_ref, cw_ref, cb_ref, wd_ref, o_ref, ubuf, *, tm):
    s = pl.program_id(1)

    @pl.when(s == 0)
    def _():
        ubuf[...] = jnp.zeros_like(ubuf)

    x = x_ref[0]
    hb = _rms(x, g2_ref[...]).astype(BF16)
    ubuf[0:SUBLANES, :] = ubuf[tm:tm + SUBLANES, :]
    ubuf[SUBLANES:SUBLANES + tm, :] = jnp.dot(hb, wu_ref[...], preferred_element_type=F32)
    conv = cb_ref[...]
    for k in range(FFN_CONV_W):
        off = SUBLANES - (FFN_CONV_W - 1) + k
        conv = conv + ubuf[off:off + tm, :] * cw_ref[k:k + 1, :]
    act = _gelu(conv) * jnp.dot(hb, wg_ref[...], preferred_element_type=F32)
    o_ref[0] = x + jnp.dot(act.astype(BF16), wd_ref[...], preferred_element_type=F32)


def _ffn(x3, g2, wu, wg, cw, cb, wd, *, tm):
    bsz, seq, _ = x3.shape
    kern = functools.partial(_ffn_kernel, tm=tm)
    return pl.pallas_call(
        kern,
        grid=(bsz, seq // tm),
        in_specs=[
            pl.BlockSpec((1, tm, D_MODEL), lambda b, s: (b, s, 0)),
            _const_spec((1, D_MODEL)),
            _const_spec((D_MODEL, D_FF)),
            _const_spec((D_MODEL, D_FF)),
            _const_spec((FFN_CONV_W, D_FF)),
            _const_spec((1, D_FF)),
            _const_spec((D_FF, D_MODEL)),
        ],
        out_specs=pl.BlockSpec((1, tm, D_MODEL), lambda b, s: (b, s, 0)),
        out_shape=jax.ShapeDtypeStruct((bsz, seq, D_MODEL), F32),
        scratch_shapes=[pltpu.VMEM((tm + SUBLANES, D_FF), F32)],
        compiler_params=pltpu.CompilerParams(
            dimension_semantics=("arbitrary", "arbitrary"), vmem_limit_bytes=VMEM_LIMIT),
        name="ffn",
    )(x3, g2, wu, wg, cw, cb, wd)


def _tile(seq, want):
    t = min(want, seq)
    assert seq % t == 0 and t % MOBA_BLOCK == 0
    return t


def kernel(x, norm1_g, w_in, conv_w, conv_b, w_r, b_r, w_i, b_i, lru_lambda, q_norm_g, k_norm_g,
           w_proj_rnn, w_proj_attn, w_out, norm2_g, w_up, w_gate, ffn_conv_w, ffn_conv_b, w_down):
    bsz, seq, d = x.shape
    assert d == D_MODEL and seq % MOBA_BLOCK == 0 and seq // MOBA_BLOCK <= LANES
    depth = norm1_g.shape[0]
    t = bsz * seq
    for l in range(depth):
        x2 = x.reshape(t, D_MODEL)
        wri = jnp.concatenate([w_r[l], w_i[l]], axis=-1).astype(BF16)
        bri = jnp.concatenate([b_r[l], b_i[l]], axis=-1)[:, None, :]
        ya, q, k, v, gates = _in_proj(
            x2, norm1_g[l][None], w_in[l].astype(BF16), q_norm_g[l][None], k_norm_g[l][None],
            conv_w[l], conv_b[l][None], wri, bri, lru_lambda[l][None], bsz=bsz, seq=seq)
        yb = _moba(q, k, v, hp=2)
        x1 = _merge(ya, yb.reshape(t, D_ATTN), gates, x2,
                    w_proj_rnn[l].astype(BF16), w_proj_attn[l].astype(BF16), w_out[l].astype(BF16),
                    tm=_tile(seq, 512))
        x = _ffn(x1.reshape(bsz, seq, D_MODEL), norm2_g[l][None], w_up[l].astype(BF16),
                 w_gate[l].astype(BF16), ffn_conv_w[l], ffn_conv_b[l][None], w_down[l].astype(BF16),
                 tm=_tile(seq, 512))
    return x
```

```python
import functools

import jax
import jax.numpy as jnp
from jax import lax
from jax.experimental import pallas as pl
from jax.experimental.pallas import tpu as pltpu

D_MODEL = 1024
D_RNN = 1024
RNN_BLOCKS = 8
RNN_BW = D_RNN // RNN_BLOCKS
CONV_W = 4
LRU_C = 8.0
N_HEADS = 8
HEAD_DIM = 128
D_ATTN = N_HEADS * HEAD_DIM
MOBA_BLOCK = 256
MOBA_TOPK = 3
D_FF = 2816
FFN_CONV_W = 3
EPS = 1e-6
NEG = -1e30
TAKEN = -3e38
LOG2_E = 1.4426950408889634

LANES = 128
SUBLANES = 8
V_ROWS = HEAD_DIM + 2 * SUBLANES
MOBA_TILE = 2 * MOBA_BLOCK
LOOP_TIERS = (8, 4, 2)
VMEM_LIMIT = 56 * 1024 * 1024

F32 = jnp.float32
BF16 = jnp.bfloat16

C_RNN = 0
C_Q = 2 * D_RNN
C_K = C_Q + D_ATTN
C_V = C_K + D_ATTN
C_G = C_V + D_ATTN
D_IN = C_G + 2 * D_MODEL


def _rms(y, g):
    return y * lax.rsqrt(jnp.mean(y * y, axis=-1, keepdims=True) + EPS) * g


def _gelu(x):
    k1 = -2.0 * 0.7978845608028654 * LOG2_E
    k2 = k1 * 0.044715
    return x * (1.0 / (1.0 + jnp.exp2(x * (k1 + k2 * (x * x)))))


def _const_spec(shape):
    nd = len(shape)
    return pl.BlockSpec(shape, lambda *_: (0,) * nd, pipeline_mode=pl.Buffered(1))


def _lru_scan(a, b, h0):
    rows, width = a.shape
    groups = rows // SUBLANES
    a = a.reshape(groups, SUBLANES, width)
    b = b.reshape(groups, SUBLANES, width)
    sub = lax.broadcasted_iota(jnp.int32, a.shape, 1)
    for d in (1, 2, 4):
        a_sh = jnp.where(sub >= d, pltpu.roll(a, d, 1), 1.0)
        b_sh = jnp.where(sub >= d, pltpu.roll(b, d, 1), 0.0)
        b = a * b_sh + b
        a = a * a_sh
    out = []
    for g in range(groups):
        hg = b[g] + a[g] * h0
        h0 = hg[SUBLANES - 1:SUBLANES]
        out.append(hg)
    return jnp.concatenate(out, axis=0), h0


def _in_proj_kernel(x_ref, g1_ref, w_ref, qg_ref, kg_ref, cw_ref, cb_ref, wri_ref, bri_ref, lam_ref,
                    ya_ref, q_ref, k_ref, v_ref, gate_ref, km_sc, xbuf, hst, hb_sc, y_sc, *, tiles_per_seq):
    tm = MOBA_BLOCK
    j = pl.program_id(0) % tiles_per_seq
    nt_dims = (((1,), (1,)), ((), ()))

    @pl.when(j == 0)
    def _():
        km_sc[...] = jnp.zeros_like(km_sc)
        xbuf[...] = jnp.zeros_like(xbuf)
        hst[...] = jnp.zeros_like(hst)

    hb_sc[...] = _rms(x_ref[...], g1_ref[...]).astype(BF16)
    pw = 4 * LANES

    y_sc[...] = jnp.dot(hb_sc[...], w_ref[...], preferred_element_type=F32)

    def proj(c0, width=pw):
        return y_sc[:, c0:c0 + width]

    xbuf[0:SUBLANES, :] = xbuf[tm:tm + SUBLANES, :]
    xbuf[SUBLANES:SUBLANES + tm, :] = proj(C_RNN, D_RNN)

    def rnn_block(n):
        cols = slice(n * RNN_BW, (n + 1) * RNN_BW)
        xa = cb_ref[:, cols]
        for k in range(CONV_W):
            off = SUBLANES - (CONV_W - 1) + k
            xa = xa + xbuf[off:off + tm, cols] * cw_ref[k:k + 1, cols]
        z = jnp.dot(xa.astype(BF16), wri_ref[n], preferred_element_type=F32) + bri_ref[n]
        r = jax.nn.sigmoid(z[:, 0:RNN_BW])
        gi = jax.nn.sigmoid(z[:, RNN_BW:2 * RNN_BW])
        a = jnp.exp(r * (-LRU_C * jax.nn.softplus(-lam_ref[:, cols])))
        w = 1.0 - a * a
        b = jnp.where(w > 0.0, w * lax.rsqrt(w), 0.0) * (gi * xa)
        hseq, hlast = _lru_scan(a, b, hst[0:1, cols])
        hst[0:1, cols] = hlast
        ya_ref[:, cols] = hseq.astype(BF16)
        ya_ref[:, D_RNN + n * RNN_BW:D_RNN + (n + 1) * RNN_BW] = proj(
            C_RNN + D_RNN + n * RNN_BW, RNN_BW).astype(BF16)

    nrow = km_sc.shape[0]
    blk = lax.broadcasted_iota(jnp.int32, (nrow, tm), 0)
    blk_f = blk.astype(F32)
    scale = HEAD_DIM ** -0.5 * LOG2_E

    def q_unit(p):
        yq = proj(C_Q + p * pw)
        for i in range(pw // HEAD_DIM):
            h = p * (pw // HEAD_DIM) + i
            cols = slice(h * HEAD_DIM, (h + 1) * HEAD_DIM)
            y = _rms(yq[:, i * HEAD_DIM:(i + 1) * HEAD_DIM], qg_ref[...]) * scale
            q_ref[0, h, 0:HEAD_DIM, :] = jnp.transpose(y).astype(BF16)
            gate_t = lax.dot_general(km_sc[:, cols], y, nt_dims,
                                     precision=lax.Precision.HIGHEST, preferred_element_type=F32)
            g = jnp.where(blk < j, gate_t, jnp.where(blk == j, TAKEN, NEG))
            for slot in range(MOBA_TOPK):
                gmax = jnp.max(g, axis=0, keepdims=True)
                idx = jnp.min(jnp.where(g == gmax, blk_f, float(LANES)), axis=0, keepdims=True)
                g = jnp.where(jnp.logical_and(blk_f == idx, slot < j), TAKEN, g)
            negsel_t = jnp.where(g == TAKEN, 0.0, NEG)
            if nrow < LANES:
                negsel_t = jnp.concatenate([negsel_t, jnp.zeros((LANES - nrow, tm), F32)], axis=0)
            q_ref[0, h, HEAD_DIM:2 * HEAD_DIM, :] = negsel_t.astype(BF16)

    lane = lax.broadcasted_iota(jnp.int32, (tm, LANES), 1)
    onehot = jnp.where(lane == j, 1.0, 0.0).astype(BF16)

    def k_unit(p):
        yk = proj(C_K + p * pw)
        for i in range(pw // HEAD_DIM):
            h = p * (pw // HEAD_DIM) + i
            cols = slice(h * HEAD_DIM, (h + 1) * HEAD_DIM)
            y = _rms(yk[:, i * HEAD_DIM:(i + 1) * HEAD_DIM], kg_ref[...])
            k_ref[0, h, :, 0:HEAD_DIM] = y.astype(BF16)
            k_ref[0, h, :, HEAD_DIM:2 * HEAD_DIM] = onehot
            km_sc[:, cols] = jnp.where(blk[:, 0:HEAD_DIM] == j, jnp.mean(y, axis=0, keepdims=True),
                                       km_sc[:, cols])

    ones = jnp.ones((V_ROWS - HEAD_DIM, tm), BF16)

    def v_unit(p):
        yv = proj(C_V + p * pw)
        for i in range(pw // HEAD_DIM):
            h = p * (pw // HEAD_DIM) + i
            v_ref[0, h, 0:HEAD_DIM, :] = jnp.transpose(yv[:, i * HEAD_DIM:(i + 1) * HEAD_DIM]).astype(BF16)
            v_ref[0, h, HEAD_DIM:V_ROWS, :] = ones

    def gate_unit(c):
        gate_ref[:, c * pw:(c + 1) * pw] = proj(C_G + c * pw).astype(BF16)

    units = ([functools.partial(q_unit, p) for p in range(D_ATTN // pw)]
             + [functools.partial(k_unit, p) for p in range(D_ATTN // pw)]
             + [functools.partial(v_unit, p) for p in range(D_ATTN // pw)]
             + [functools.partial(gate_unit, c) for c in range(2 * D_MODEL // pw)])
    for n in range(RNN_BLOCKS):
        rnn_block(n)
        if n < len(units):
            units[n]()
    for unit in units[RNN_BLOCKS:]:
        unit()


def _in_proj(x2, g1, w_in, qg, kg, cw, cb, wri, bri, lam, *, bsz, seq):
    t = bsz * seq
    tm = MOBA_BLOCK
    nst = seq // tm
    nrow = -(-nst // SUBLANES) * SUBLANES
    kern = functools.partial(_in_proj_kernel, tiles_per_seq=nst)
    return pl.pallas_call(
        kern,
        grid=(t // tm,),
        in_specs=[
            pl.BlockSpec((tm, D_MODEL), lambda i: (i, 0)),
            _const_spec((1, D_MODEL)),
            _const_spec((D_MODEL, D_IN)),
            _const_spec((1, HEAD_DIM)),
            _const_spec((1, HEAD_DIM)),
            _const_spec((CONV_W, D_RNN)),
            _const_spec((1, D_RNN)),
            _const_spec((RNN_BLOCKS, RNN_BW, 2 * RNN_BW)),
            _const_spec((RNN_BLOCKS, 1, 2 * RNN_BW)),
            _const_spec((1, D_RNN)),
        ],
        out_specs=[
            pl.BlockSpec((tm, 2 * D_RNN), lambda i: (i, 0)),
            pl.BlockSpec((1, N_HEADS, 2 * HEAD_DIM, tm), lambda i: (i // nst, 0, 0, i % nst)),
            pl.BlockSpec((1, N_HEADS, tm, 2 * HEAD_DIM), lambda i: (i // nst, 0, i % nst, 0)),
            pl.BlockSpec((1, N_HEADS, V_ROWS, tm), lambda i: (i // nst, 0, 0, i % nst)),
            pl.BlockSpec((tm, 2 * D_MODEL), lambda i: (i, 0)),
        ],
        out_shape=[
            jax.ShapeDtypeStruct((t, 2 * D_RNN), BF16),
            jax.ShapeDtypeStruct((bsz, N_HEADS, 2 * HEAD_DIM, seq), BF16),
            jax.ShapeDtypeStruct((bsz, N_HEADS, seq, 2 * HEAD_DIM), BF16),
            jax.ShapeDtypeStruct((bsz, N_HEADS, V_ROWS, seq), BF16),
            jax.ShapeDtypeStruct((t, 2 * D_MODEL), BF16),
        ],
        scratch_shapes=[
            pltpu.VMEM((nrow, D_ATTN), F32),
            pltpu.VMEM((tm + SUBLANES, D_RNN), F32),
            pltpu.VMEM((SUBLANES, D_RNN), F32),
            pltpu.VMEM((tm, D_MODEL), BF16),
            pltpu.VMEM((tm, D_IN), F32),
        ],
        compiler_params=pltpu.CompilerParams(
            dimension_semantics=("arbitrary",), vmem_limit_bytes=VMEM_LIMIT),
        name="in_proj",
    )(x2, g1, w_in, qg, kg, cw, cb, wri, bri, lam)


def _moba_kernel(q_ref, qn_ref, k_ref, v_ref, o_ref, sa_sc, sb_sc, xa_sc, xb_sc, m_sc, acc_sc, *, hp):
    t = pl.program_id(2)
    kc = MOBA_TILE
    heads = range(hp)

    def scores(h, chunk, dst, q=q_ref):
        st = pl.multiple_of(chunk * kc, kc)
        sc = jnp.dot(k_ref[0, h, pl.ds(st, kc), :], q[0, h], preferred_element_type=F32)
        dst[0][h] = sc
        dst[1][h] = jnp.max(sc, axis=0, keepdims=True)

    def accumulate(h, chunk, src, causal=False):
        st = pl.multiple_of(chunk * kc, kc)
        sc = src[0][h]
        if causal:
            key_i = lax.broadcasted_iota(jnp.int32, sc.shape, 0)
            qry_i = lax.broadcasted_iota(jnp.int32, sc.shape, 1)
            sc = jnp.where(key_i <= qry_i, sc, NEG)
        m_prev = m_sc[h]
        m_new = jnp.maximum(m_prev, src[1][h])
        p = jnp.exp2(sc - m_new).astype(BF16)
        acc_sc[h] = jnp.exp2(m_prev - m_new) * acc_sc[h] + jnp.dot(
            v_ref[0, h, :, pl.ds(st, kc)], p, preferred_element_type=F32)
        m_sc[h] = m_new

    m_sc[...] = jnp.full_like(m_sc, NEG)
    acc_sc[...] = jnp.zeros_like(acc_sc)
    buf_a, buf_b = (sa_sc, xa_sc), (sb_sc, xb_sc)

    @pl.when(t == 0)
    def _():
        for h in heads:
            scores(h, 0, buf_a)

    def step(c, cur, nxt):
        for h in heads:
            scores(h, c + 1, nxt)
        for h in heads:
            accumulate(h, c, cur)

    def run(first, trips, unroll):
        def body(i, carry):
            for u in range(unroll):
                cur, nxt = (buf_a, buf_b) if u % 2 == 0 else (buf_b, buf_a)
                step(first + unroll * i + u, cur, nxt)
            return carry

        lax.fori_loop(0, trips, body, 0)

    done, rest = 0, t
    for unroll in LOOP_TIERS:
        trips = lax.div(rest, unroll)
        run(done, trips, unroll)
        done = done + trips * unroll
        rest = rest - trips * unroll

    def finish(own_buf):
        for h in heads:
            accumulate(h, t, own_buf, causal=True)
        for h in heads:
            scores(h, 0, buf_a, qn_ref)
        for h in heads:
            acc = acc_sc[h]
            o_t = acc[0:HEAD_DIM, :] / acc[HEAD_DIM:HEAD_DIM + 1, :]
            o_ref[0, :, h * HEAD_DIM:(h + 1) * HEAD_DIM] = jnp.transpose(o_t).astype(BF16)

    @pl.when(rest % 2 == 1)
    def _():
        step(t - 1, buf_a, buf_b)
        finish(buf_b)

    @pl.when(rest % 2 == 0)
    def _():
        finish(buf_a)


def _moba(q, k, v, *, hp):
    bsz, nh, _, seq = q.shape
    assert seq % MOBA_TILE == 0 and nh % hp == 0
    kc = MOBA_TILE
    last = seq // MOBA_TILE - 1
    kern = functools.partial(_moba_kernel, hp=hp)
    return pl.pallas_call(
        kern,
        grid=(bsz, nh // hp, seq // MOBA_TILE),
        in_specs=[
            pl.BlockSpec((1, hp, 2 * HEAD_DIM, MOBA_TILE), lambda b, h, j: (b, h, 0, j)),
            pl.BlockSpec((1, hp, 2 * HEAD_DIM, MOBA_TILE), lambda b, h, j: (b, h, 0, jnp.minimum(j + 1, last))),
            pl.BlockSpec((1, hp, seq, 2 * HEAD_DIM), lambda b, h, j: (b, h, 0, 0)),
            pl.BlockSpec((1, hp, V_ROWS, seq), lambda b, h, j: (b, h, 0, 0)),
        ],
        out_specs=pl.BlockSpec((1, MOBA_TILE, hp * HEAD_DIM), lambda b, h, j: (b, j, h)),
        out_shape=jax.ShapeDtypeStruct((bsz, seq, D_ATTN), BF16),
        scratch_shapes=[
            pltpu.VMEM((hp, kc, MOBA_TILE), F32),
            pltpu.VMEM((hp, kc, MOBA_TILE), F32),
            pltpu.VMEM((hp, 1, MOBA_TILE), F32),
            pltpu.VMEM((hp, 1, MOBA_TILE), F32),
            pltpu.VMEM((hp, 1, MOBA_TILE), F32),
            pltpu.VMEM((hp, V_ROWS, MOBA_TILE), F32),
        ],
        compiler_params=pltpu.CompilerParams(
            dimension_semantics=("arbitrary", "arbitrary", "arbitrary"), vmem_limit_bytes=VMEM_LIMIT),
        name="moba",
    )(q, q, k, v)


def _merge_kernel(ya_ref, yb_ref, gate_ref, x_ref, pa_ref, pb_ref, wo_ref, o_ref):
    ya = (ya_ref[:, 0:D_RNN].astype(F32) * _gelu(ya_ref[:, D_RNN:2 * D_RNN].astype(F32))).astype(BF16)
    a = jnp.dot(ya, pa_ref[...], preferred_element_type=F32)
    b = jnp.dot(yb_ref[...], pb_ref[...], preferred_element_type=F32)
    merged = (jax.nn.sigmoid(gate_ref[:, 0:D_MODEL].astype(F32)) * a
              + jax.nn.sigmoid(gate_ref[:, D_MODEL:2 * D_MODEL].astype(F32)) * b)
    o_ref[...] = x_ref[...] + jnp.dot(merged.astype(BF16), wo_ref[...], preferred_element_type=F32)


def _merge(ya, yb, gates, x2, pa, pb, wo, *, tm):
    t = x2.shape[0]
    row = lambda w: pl.BlockSpec((tm, w), lambda i: (i, 0))
    return pl.pallas_call(
        _merge_kernel,
        grid=(t // tm,),
        in_specs=[row(2 * D_RNN), row(D_ATTN), row(2 * D_MODEL), row(D_MODEL),
                  _const_spec((D_RNN, D_MODEL)), _const_spec((D_ATTN, D_MODEL)),
                  _const_spec((D_MODEL, D_MODEL))],
        out_specs=row(D_MODEL),
        out_shape=jax.ShapeDtypeStruct((t, D_MODEL), F32),
        compiler_params=pltpu.CompilerParams(
            dimension_semantics=("arbitrary",), vmem_limit_bytes=VMEM_LIMIT),
        name="merge",
    )(ya, yb, gates, x2, pa, pb, wo)


def _ffn_kernel(x_ref, g2_ref, wu_ref, wg_ref, cw_ref, cb_ref, wd_ref, o_ref, ubuf, *, tm):
    s = pl.program_id(1)

    @pl.when(s == 0)
    def _():
        ubuf[...] = jnp.zeros_like(ubuf)

    x = x_ref[0]
    hb = _rms(x, g2_ref[...]).astype(BF16)
    ubuf[0:SUBLANES, :] = ubuf[tm:tm + SUBLANES, :]
    ubuf[SUBLANES:SUBLANES + tm, :] = jnp.dot(hb, wu_ref[...], preferred_element_type=F32)
    conv = cb_ref[...]
    for k in range(FFN_CONV_W):
        off = SUBLANES - (FFN_CONV_W - 1) + k
        conv = conv + ubuf[off:off + tm, :] * cw_ref[k:k + 1, :]
    act = _gelu(conv) * jnp.dot(hb, wg_ref[...], preferred_element_type=F32)
    o_ref[0] = x + jnp.dot(act.astype(BF16), wd_ref[...], preferred_element_type=F32)


def _ffn(x3, g2, wu, wg, cw, cb, wd, *, tm):
    bsz, seq, _ = x3.shape
    kern = functools.partial(_ffn_kernel, tm=tm)
    return pl.pallas_call(
        kern,
        grid=(bsz, seq // tm),
        in_specs=[
            pl.BlockSpec((1, tm, D_MODEL), lambda b, s: (b, s, 0)),
            _const_spec((1, D_MODEL)),
            _const_spec((D_MODEL, D_FF)),
            _const_spec((D_MODEL, D_FF)),
            _const_spec((FFN_CONV_W, D_FF)),
            _const_spec((1, D_FF)),
            _const_spec((D_FF, D_MODEL)),
        ],
        out_specs=pl.BlockSpec((1, tm, D_MODEL), lambda b, s: (b, s, 0)),
        out_shape=jax.ShapeDtypeStruct((bsz, seq, D_MODEL), F32),
        scratch_shapes=[pltpu.VMEM((tm + SUBLANES, D_FF), F32)],
        compiler_params=pltpu.CompilerParams(
            dimension_semantics=("arbitrary", "arbitrary"), vmem_limit_bytes=VMEM_LIMIT),
        name="ffn",
    )(x3, g2, wu, wg, cw, cb, wd)


def _tile(seq, want):
    t = min(want, seq)
    assert seq % t == 0 and t % MOBA_BLOCK == 0
    return t


def kernel(x, norm1_g, w_in, conv_w, conv_b, w_r, b_r, w_i, b_i, lru_lambda, q_norm_g, k_norm_g,
           w_proj_rnn, w_proj_attn, w_out, norm2_g, w_up, w_gate, ffn_conv_w, ffn_conv_b, w_down):
    bsz, seq, d = x.shape
    assert d == D_MODEL and seq % MOBA_BLOCK == 0 and seq // MOBA_BLOCK <= LANES
    depth = norm1_g.shape[0]
    t = bsz * seq
    for l in range(depth):
        x2 = x.reshape(t, D_MODEL)
        wri = jnp.concatenate([w_r[l], w_i[l]], axis=-1).astype(BF16)
        bri = jnp.concatenate([b_r[l], b_i[l]], axis=-1)[:, None, :]
        ya, q, k, v, gates = _in_proj(
            x2, norm1_g[l][None], w_in[l].astype(BF16), q_norm_g[l][None], k_norm_g[l][None],
            conv_w[l], conv_b[l][None], wri, bri, lru_lambda[l][None], bsz=bsz, seq=seq)
        yb = _moba(q, k, v, hp=2)
        x1 = _merge(ya, yb.reshape(t, D_ATTN), gates, x2,
                    w_proj_rnn[l].astype(BF16), w_proj_attn[l].astype(BF16), w_out[l].astype(BF16),
                    tm=_tile(seq, 512))
        x = _ffn(x1.reshape(bsz, seq, D_MODEL), norm2_g[l][None], w_up[l].astype(BF16),
                 w_gate[l].astype(BF16), ffn_conv_w[l], ffn_conv_b[l][None], w_down[l].astype(BF16),
                 tm=_tile(seq, 512))
    return x
```

```python
import functools

import jax
import jax.numpy as jnp
from jax import lax
from jax.experimental import pallas as pl
from jax.experimental.pallas import tpu as pltpu

D_MODEL = 1024
D_RNN = 1024
RNN_BLOCKS = 8
RNN_BW = D_RNN // RNN_BLOCKS
CONV_W = 4
LRU_C = 8.0
N_HEADS = 8
HEAD_DIM = 128
D_ATTN = N_HEADS * HEAD_DIM
MOBA_BLOCK = 256
MOBA_TOPK = 3
D_FF = 2816
FFN_CONV_W = 3
EPS = 1e-6
NEG = -1e30
TAKEN = -3e38
LOG2_E = 1.4426950408889634

LANES = 128
SUBLANES = 8
V_ROWS = HEAD_DIM + 2 * SUBLANES
MOBA_TILE = 2 * MOBA_BLOCK
LONG_BODY = 8
VMEM_LIMIT = 56 * 1024 * 1024

F32 = jnp.float32
BF16 = jnp.bfloat16

C_RNN = 0
C_Q = 2 * D_RNN
C_K = C_Q + D_ATTN
C_V = C_K + D_ATTN
C_G = C_V + D_ATTN
D_IN = C_G + 2 * D_MODEL


def _rms(y, g):
    return y * lax.rsqrt(jnp.mean(y * y, axis=-1, keepdims=True) + EPS) * g


def _gelu(x):
    k1 = -2.0 * 0.7978845608028654 * LOG2_E
    k2 = k1 * 0.044715
    return x * (1.0 / (1.0 + jnp.exp2(x * (k1 + k2 * (x * x)))))


def _const_spec(shape):
    nd = len(shape)
    return pl.BlockSpec(shape, lambda *_: (0,) * nd, pipeline_mode=pl.Buffered(1))


def _lru_scan(a, b, h0):
    rows, width = a.shape
    groups = rows // SUBLANES
    a = a.reshape(groups, SUBLANES, width)
    b = b.reshape(groups, SUBLANES, width)
    sub = lax.broadcasted_iota(jnp.int32, a.shape, 1)
    for d in (1, 2, 4):
        a_sh = jnp.where(sub >= d, pltpu.roll(a, d, 1), 1.0)
        b_sh = jnp.where(sub >= d, pltpu.roll(b, d, 1), 0.0)
        b = a * b_sh + b
        a = a * a_sh
    out = []
    for g in range(groups):
        hg = b[g] + a[g] * h0
        h0 = hg[SUBLANES - 1:SUBLANES]
        out.append(hg)
    return jnp.concatenate(out, axis=0), h0


def _in_proj_kernel(x_ref, g1_ref, w_ref, qg_ref, kg_ref, cw_ref, cb_ref, wri_ref, bri_ref, lam_ref,
                    ya_ref, q_ref, k_ref, v_ref, gate_ref, km_sc, xbuf, hst, hb_sc, y_sc, *, tiles_per_seq):
    tm = MOBA_BLOCK
    j = pl.program_id(0) % tiles_per_seq
    nt_dims = (((1,), (1,)), ((), ()))

    @pl.when(j == 0)
    def _():
        km_sc[...] = jnp.zeros_like(km_sc)
        xbuf[...] = jnp.zeros_like(xbuf)
        hst[...] = jnp.zeros_like(hst)

    hb_sc[...] = _rms(x_ref[...], g1_ref[...]).astype(BF16)
    pw = 4 * LANES

    y_sc[...] = jnp.dot(hb_sc[...], w_ref[...], preferred_element_type=F32)

    def proj(c0, width=pw):
        return y_sc[:, c0:c0 + width]

    xbuf[0:SUBLANES, :] = xbuf[tm:tm + SUBLANES, :]
    xbuf[SUBLANES:SUBLANES + tm, :] = proj(C_RNN, D_RNN)

    def rnn_block(n):
        cols = slice(n * RNN_BW, (n + 1) * RNN_BW)
        xa = cb_ref[:, cols]
        for k in range(CONV_W):
            off = SUBLANES - (CONV_W - 1) + k
            xa = xa + xbuf[off:off + tm, cols] * cw_ref[k:k + 1, cols]
        z = jnp.dot(xa.astype(BF16), wri_ref[n], preferred_element_type=F32) + bri_ref[n]
        r = jax.nn.sigmoid(z[:, 0:RNN_BW])
        gi = jax.nn.sigmoid(z[:, RNN_BW:2 * RNN_BW])
        a = jnp.exp(r * (-LRU_C * jax.nn.softplus(-lam_ref[:, cols])))
        w = 1.0 - a * a
        b = jnp.where(w > 0.0, w * lax.rsqrt(w), 0.0) * (gi * xa)
        hseq, hlast = _lru_scan(a, b, hst[0:1, cols])
        hst[0:1, cols] = hlast
        ya_ref[:, cols] = hseq.astype(BF16)
        ya_ref[:, D_RNN + n * RNN_BW:D_RNN + (n + 1) * RNN_BW] = proj(
            C_RNN + D_RNN + n * RNN_BW, RNN_BW).astype(BF16)

    nrow = km_sc.shape[0]
    blk = lax.broadcasted_iota(jnp.int32, (nrow, tm), 0)
    blk_f = blk.astype(F32)
    scale = HEAD_DIM ** -0.5 * LOG2_E

    def q_unit(p):
        yq = proj(C_Q + p * pw)
        for i in range(pw // HEAD_DIM):
            h = p * (pw // HEAD_DIM) + i
            cols = slice(h * HEAD_DIM, (h + 1) * HEAD_DIM)
            y = _rms(yq[:, i * HEAD_DIM:(i + 1) * HEAD_DIM], qg_ref[...]) * scale
            q_ref[0, h, 0:HEAD_DIM, :] = jnp.transpose(y).astype(BF16)
            gate_t = lax.dot_general(km_sc[:, cols], y, nt_dims,
                                     precision=lax.Precision.HIGHEST, preferred_element_type=F32)
            g = jnp.where(blk < j, gate_t, jnp.where(blk == j, TAKEN, NEG))
            for slot in range(MOBA_TOPK):
                gmax = jnp.max(g, axis=0, keepdims=True)
                idx = jnp.min(jnp.where(g == gmax, blk_f, float(LANES)), axis=0, keepdims=True)
                g = jnp.where(jnp.logical_and(blk_f == idx, slot < j), TAKEN, g)
            negsel_t = jnp.where(g == TAKEN, 0.0, NEG)
            if nrow < LANES:
                negsel_t = jnp.concatenate([negsel_t, jnp.zeros((LANES - nrow, tm), F32)], axis=0)
            q_ref[0, h, HEAD_DIM:2 * HEAD_DIM, :] = negsel_t.astype(BF16)

    lane = lax.broadcasted_iota(jnp.int32, (tm, LANES), 1)
    onehot = jnp.where(lane == j, 1.0, 0.0).astype(BF16)

    def k_unit(p):
        yk = proj(C_K + p * pw)
        for i in range(pw // HEAD_DIM):
            h = p * (pw // HEAD_DIM) + i
            cols = slice(h * HEAD_DIM, (h + 1) * HEAD_DIM)
            y = _rms(yk[:, i * HEAD_DIM:(i + 1) * HEAD_DIM], kg_ref[...])
            k_ref[0, h, :, 0:HEAD_DIM] = y.astype(BF16)
            k_ref[0, h, :, HEAD_DIM:2 * HEAD_DIM] = onehot
            km_sc[:, cols] = jnp.where(blk[:, 0:HEAD_DIM] == j, jnp.mean(y, axis=0, keepdims=True),
                                       km_sc[:, cols])

    ones = jnp.ones((V_ROWS - HEAD_DIM, tm), BF16)

    def v_unit(p):
        yv = proj(C_V + p * pw)
        for i in range(pw // HEAD_DIM):
            h = p * (pw // HEAD_DIM) + i
            v_ref[0, h, 0:HEAD_DIM, :] = jnp.transpose(yv[:, i * HEAD_DIM:(i + 1) * HEAD_DIM]).astype(BF16)
            v_ref[0, h, HEAD_DIM:V_ROWS, :] = ones

    def gate_unit(c):
        gate_ref[:, c * pw:(c + 1) * pw] = proj(C_G + c * pw).astype(BF16)

    units = ([functools.partial(q_unit, p) for p in range(D_ATTN // pw)]
             + [functools.partial(k_unit, p) for p in range(D_ATTN // pw)]
             + [functools.partial(v_unit, p) for p in range(D_ATTN // pw)]
             + [functools.partial(gate_unit, c) for c in range(2 * D_MODEL // pw)])
    for n in range(RNN_BLOCKS):
        rnn_block(n)
        if n < len(units):
            units[n]()
    for unit in units[RNN_BLOCKS:]:
        unit()


def _in_proj(x2, g1, w_in, qg, kg, cw, cb, wri, bri, lam, *, bsz, seq):
    t = bsz * seq
    tm = MOBA_BLOCK
    nst = seq // tm
    nrow = -(-nst // SUBLANES) * SUBLANES
    kern = functools.partial(_in_proj_kernel, tiles_per_seq=nst)
    return pl.pallas_call(
        kern,
        grid=(t // tm,),
        in_specs=[
            pl.BlockSpec((tm, D_MODEL), lambda i: (i, 0)),
            _const_spec((1, D_MODEL)),
            _const_spec((D_MODEL, D_IN)),
            _const_spec((1, HEAD_DIM)),
            _const_spec((1, HEAD_DIM)),
            _const_spec((CONV_W, D_RNN)),
            _const_spec((1, D_RNN)),
            _const_spec((RNN_BLOCKS, RNN_BW, 2 * RNN_BW)),
            _const_spec((RNN_BLOCKS, 1, 2 * RNN_BW)),
            _const_spec((1, D_RNN)),
        ],
        out_specs=[
            pl.BlockSpec((tm, 2 * D_RNN), lambda i: (i, 0)),
            pl.BlockSpec((1, N_HEADS, 2 * HEAD_DIM, tm), lambda i: (i // nst, 0, 0, i % nst)),
            pl.BlockSpec((1, N_HEADS, tm, 2 * HEAD_DIM), lambda i: (i // nst, 0, i % nst, 0)),
            pl.BlockSpec((1, N_HEADS, V_ROWS, tm), lambda i: (i // nst, 0, 0, i % nst)),
            pl.BlockSpec((tm, 2 * D_MODEL), lambda i: (i, 0)),
        ],
        out_shape=[
            jax.ShapeDtypeStruct((t, 2 * D_RNN), BF16),
            jax.ShapeDtypeStruct((bsz, N_HEADS, 2 * HEAD_DIM, seq), BF16),
            jax.ShapeDtypeStruct((bsz, N_HEADS, seq, 2 * HEAD_DIM), BF16),
            jax.ShapeDtypeStruct((bsz, N_HEADS, V_ROWS, seq), BF16),
            jax.ShapeDtypeStruct((t, 2 * D_MODEL), BF16),
        ],
        scratch_shapes=[
            pltpu.VMEM((nrow, D_ATTN), F32),
            pltpu.VMEM((tm + SUBLANES, D_RNN), F32),
            pltpu.VMEM((SUBLANES, D_RNN), F32),
            pltpu.VMEM((tm, D_MODEL), BF16),
            pltpu.VMEM((tm, D_IN), F32),
        ],
        compiler_params=pltpu.CompilerParams(
            dimension_semantics=("arbitrary",), vmem_limit_bytes=VMEM_LIMIT),
        name="in_proj",
    )(x2, g1, w_in, qg, kg, cw, cb, wri, bri, lam)


def _moba_kernel(q_ref, qn_ref, k_ref, v_ref, o_ref, sa_sc, sb_sc, xa_sc, xb_sc, m_sc, acc_sc, *, hp):
    t = pl.program_id(2)
    kc = MOBA_TILE
    heads = range(hp)

    def scores(h, chunk, dst, q=q_ref):
        st = pl.multiple_of(chunk * kc, kc)
        sc = jnp.dot(k_ref[0, h, pl.ds(st, kc), :], q[0, h], preferred_element_type=F32)
        dst[0][h] = sc
        dst[1][h] = jnp.max(sc, axis=0, keepdims=True)

    def accumulate(h, chunk, src, causal=False):
        st = pl.multiple_of(chunk * kc, kc)
        sc = src[0][h]
        if causal:
            key_i = lax.broadcasted_iota(jnp.int32, sc.shape, 0)
            qry_i = lax.broadcasted_iota(jnp.int32, sc.shape, 1)
            sc = jnp.where(key_i <= qry_i, sc, NEG)
        m_prev = m_sc[h]
        m_new = jnp.maximum(m_prev, src[1][h])
        p = jnp.exp2(sc - m_new).astype(BF16)
        acc_sc[h] = jnp.exp2(m_prev - m_new) * acc_sc[h] + jnp.dot(
            v_ref[0, h, :, pl.ds(st, kc)], p, preferred_element_type=F32)
        m_sc[h] = m_new

    m_sc[...] = jnp.full_like(m_sc, NEG)
    acc_sc[...] = jnp.zeros_like(acc_sc)
    buf_a, buf_b = (sa_sc, xa_sc), (sb_sc, xb_sc)

    @pl.when(t == 0)
    def _():
        for h in heads:
            scores(h, 0, buf_a)

    def step(c, cur, nxt):
        for h in heads:
            scores(h, c + 1, nxt)
        for h in heads:
            accumulate(h, c, cur)

    def run(first, trips, unroll):
        def body(i, carry):
            for u in range(unroll):
                cur, nxt = (buf_a, buf_b) if u % 2 == 0 else (buf_b, buf_a)
                step(first + unroll * i + u, cur, nxt)
            return carry

        lax.fori_loop(0, trips, body, 0)

    trips = lax.div(t, LONG_BODY)
    done = trips * LONG_BODY
    rest = t - done
    run(0, trips, LONG_BODY)

    def finish(own_buf):
        for h in heads:
            accumulate(h, t, own_buf, causal=True)
        for h in heads:
            scores(h, 0, buf_a, qn_ref)
        for h in heads:
            acc = acc_sc[h]
            o_t = acc[0:HEAD_DIM, :] / acc[HEAD_DIM:HEAD_DIM + 1, :]
            o_ref[0, :, h * HEAD_DIM:(h + 1) * HEAD_DIM] = jnp.transpose(o_t).astype(BF16)

    for left in range(LONG_BODY):
        @pl.when(rest == left)
        def _(left=left):
            for u in range(left):
                cur, nxt = (buf_a, buf_b) if u % 2 == 0 else (buf_b, buf_a)
                step(done + u, cur, nxt)
            finish(buf_a if left % 2 == 0 else buf_b)


def _moba(q, k, v, *, hp):
    bsz, nh, _, seq = q.shape
    assert seq % MOBA_TILE == 0 and nh % hp == 0
    kc = MOBA_TILE
    last = seq // MOBA_TILE - 1
    kern = functools.partial(_moba_kernel, hp=hp)
    return pl.pallas_call(
        kern,
        grid=(bsz, nh // hp, seq // MOBA_TILE),
        in_specs=[
            pl.BlockSpec((1, hp, 2 * HEAD_DIM, MOBA_TILE), lambda b, h, j: (b, h, 0, j)),
            pl.BlockSpec((1, hp, 2 * HEAD_DIM, MOBA_TILE), lambda b, h, j: (b, h, 0, jnp.minimum(j + 1, last))),
            pl.BlockSpec((1, hp, seq, 2 * HEAD_DIM), lambda b, h, j: (b, h, 0, 0)),
            pl.BlockSpec((1, hp, V_ROWS, seq), lambda b, h, j: (b, h, 0, 0)),
        ],
        out_specs=pl.BlockSpec((1, MOBA_TILE, hp * HEAD_DIM), lambda b, h, j: (b, j, h)),
        out_shape=jax.ShapeDtypeStruct((bsz, seq, D_ATTN), BF16),
        scratch_shapes=[
            pltpu.VMEM((hp, kc, MOBA_TILE), F32),
            pltpu.VMEM((hp, kc, MOBA_TILE), F32),
            pltpu.VMEM((hp, 1, MOBA_TILE), F32),
            pltpu.VMEM((hp, 1, MOBA_TILE), F32),
            pltpu.VMEM((hp, 1, MOBA_TILE), F32),
            pltpu.VMEM((hp, V_ROWS, MOBA_TILE), F32),
        ],
        compiler_params=pltpu.CompilerParams(
            dimension_semantics=("arbitrary", "arbitrary", "arbitrary"), vmem_limit_bytes=VMEM_LIMIT),
        name="moba",
    )(q, q, k, v)


def _merge_kernel(ya_ref, yb_ref, gate_ref, x_ref, pa_ref, pb_ref, wo_ref, o_ref):
    ya = (ya_ref[:, 0:D_RNN].astype(F32) * _gelu(ya_ref[:, D_RNN:2 * D_RNN].astype(F32))).astype(BF16)
    a = jnp.dot(ya, pa_ref[...], preferred_element_type=F32)
    b = jnp.dot(yb_ref[...], pb_ref[...], preferred_element_type=F32)
    merged = (jax.nn.sigmoid(gate_ref[:, 0:D_MODEL].astype(F32)) * a
              + jax.nn.sigmoid(gate_ref[:, D_MODEL:2 * D_MODEL].astype(F32)) * b)
    o_ref[...] = x_ref[...] + jnp.dot(merged.astype(BF16), wo_ref[...], preferred_element_type=F32)


def _merge(ya, yb, gates, x2, pa, pb, wo, *, tm):
    t = x2.shape[0]
    row = lambda w: pl.BlockSpec((tm, w), lambda i: (i, 0))
    return pl.pallas_call(
        _merge_kernel,
        grid=(t // tm,),
        in_specs=[row(2 * D_RNN), row(D_ATTN), row(2 * D_MODEL), row(D_MODEL),
                  _const_spec((D_RNN, D_MODEL)), _const_spec((D_ATTN, D_MODEL)),
                  _const_spec((D_MODEL, D_MODEL))],
        out_specs=row(D_MODEL),
        out_shape=jax.ShapeDtypeStruct((t, D_MODEL), F32),
        compiler_params=pltpu.CompilerParams(
            dimension_semantics=("arbitrary",), vmem_limit_bytes=VMEM_LIMIT),
        name="merge",
    )(ya, yb, gates, x2, pa, pb, wo)


def _ffn_kernel(x_ref, g2_ref, wu_ref, wg_ref, cw_ref, cb_ref, wd_ref, o_ref, ubuf, *, tm):
    s = pl.program_id(1)

    @pl.when(s == 0)
    def _():
        ubuf[...] = jnp.zeros_like(ubuf)

    x = x_ref[0]
    hb = _rms(x, g2_ref[...]).astype(BF16)
    ubuf[0:SUBLANES, :] = ubuf[tm:tm + SUBLANES, :]
    ubuf[SUBLANES:SUBLANES + tm, :] = jnp.dot(hb, wu_ref[...], preferred_element_type=F32)
    conv = cb_ref[...]
    for k in range(FFN_CONV_W):
        off = SUBLANES - (FFN_CONV_W - 1) + k
        conv = conv + ubuf[off:off + tm, :] * cw_ref[k:k + 1, :]
    act = _gelu(conv) * jnp.dot(hb, wg_ref[...], preferred_element_type=F32)
    o_ref[0] = x + jnp.dot(act.astype(BF16), wd_ref[...], preferred_element_type=F32)


def _ffn(x3, g2, wu, wg, cw, cb, wd, *, tm):
    bsz, seq, _ = x3.shape
    kern = functools.partial(_ffn_kernel, tm=tm)
    return pl.pallas_call(
        kern,
        grid=(bsz, seq // tm),
        in_specs=[
            pl.BlockSpec((1, tm, D_MODEL), lambda b, s: (b, s, 0)),
            _const_spec((1, D_MODEL)),
            _const_spec((D_MODEL, D_FF)),
            _const_spec((D_MODEL, D_FF)),
            _const_spec((FFN_CONV_W, D_FF)),
            _const_spec((1, D_FF)),
            _const_spec((D_FF, D_MODEL)),
        ],
        out_specs=pl.BlockSpec((1, tm, D_MODEL), lambda b, s: (b, s, 0)),
        out_shape=jax.ShapeDtypeStruct((bsz, seq, D_MODEL), F32),
        scratch_shapes=[pltpu.VMEM((tm + SUBLANES, D_FF), F32)],
        compiler_params=pltpu.CompilerParams(
            dimension_semantics=("arbitrary", "arbitrary"), vmem_limit_bytes=VMEM_LIMIT),
        name="ffn",
    )(x3, g2, wu, wg, cw, cb, wd)


def _tile(seq, want):
    t = min(want, seq)
    assert seq % t == 0 and t % MOBA_BLOCK == 0
    return t


def kernel(x, norm1_g, w_in, conv_w, conv_b, w_r, b_r, w_i, b_i, lru_lambda, q_norm_g, k_norm_g,
           w_proj_rnn, w_proj_attn, w_out, norm2_g, w_up, w_gate, ffn_conv_w, ffn_conv_b, w_down):
    bsz, seq, d = x.shape
    assert d == D_MODEL and seq % MOBA_BLOCK == 0 and seq // MOBA_BLOCK <= LANES
    depth = norm1_g.shape[0]
    t = bsz * seq
    for l in range(depth):
        x2 = x.reshape(t, D_MODEL)
        wri = jnp.concatenate([w_r[l], w_i[l]], axis=-1).astype(BF16)
        bri = jnp.concatenate([b_r[l], b_i[l]], axis=-1)[:, None, :]
        ya, q, k, v, gates = _in_proj(
            x2, norm1_g[l][None], w_in[l].astype(BF16), q_norm_g[l][None], k_norm_g[l][None],
            conv_w[l], conv_b[l][None], wri, bri, lru_lambda[l][None], bsz=bsz, seq=seq)
        yb = _moba(q, k, v, hp=2)
        x1 = _merge(ya, yb.reshape(t, D_ATTN), gates, x2,
                    w_proj_rnn[l].astype(BF16), w_proj_attn[l].astype(BF16), w_out[l].astype(BF16),
                    tm=_tile(seq, 512))
        x = _ffn(x1.reshape(bsz, seq, D_MODEL), norm2_g[l][None], w_up[l].astype(BF16),
                 w_gate[l].astype(BF16), ffn_conv_w[l], ffn_conv_b[l][None], w_down[l].astype(BF16),
                 tm=_tile(seq, 512))
    return x
```

```python
import functools

import jax
import jax.numpy as jnp
from jax import lax
from jax.experimental import pallas as pl
from jax.experimental.pallas import tpu as pltpu

D_MODEL = 1024
D_RNN = 1024
RNN_BLOCKS = 8
RNN_BW = D_RNN // RNN_BLOCKS
CONV_W = 4
LRU_C = 8.0
N_HEADS = 8
HEAD_DIM = 128
D_ATTN = N_HEADS * HEAD_DIM
MOBA_BLOCK = 256
MOBA_TOPK = 3
D_FF = 2816
FFN_CONV_W = 3
EPS = 1e-6
NEG = -1e30
TAKEN = -3e38
LOG2_E = 1.4426950408889634

LANES = 128
SUBLANES = 8
V_ROWS = HEAD_DIM + 2 * SUBLANES
MOBA_TILE = 2 * MOBA_BLOCK
LONG_BODY = 8
VMEM_LIMIT = 56 * 1024 * 1024

F32 = jnp.float32
BF16 = jnp.bfloat16

C_RNN = 0
C_Q = 2 * D_RNN
C_K = C_Q + D_ATTN
C_V = C_K + D_ATTN
C_G = C_V + D_ATTN
D_IN = C_G + 2 * D_MODEL


def _rms(y, g):
    return y * lax.rsqrt(jnp.mean(y * y, axis=-1, keepdims=True) + EPS) * g


def _gelu(x):
    k1 = -2.0 * 0.7978845608028654 * LOG2_E
    k2 = k1 * 0.044715
    return x * (1.0 / (1.0 + jnp.exp2(x * (k1 + k2 * (x * x)))))


def _const_spec(shape):
    nd = len(shape)
    return pl.BlockSpec(shape, lambda *_: (0,) * nd, pipeline_mode=pl.Buffered(1))


def _lru_scan(a, b, h0):
    rows, width = a.shape
    groups = rows // SUBLANES
    a = a.reshape(groups, SUBLANES, width)
    b = b.reshape(groups, SUBLANES, width)
    sub = lax.broadcasted_iota(jnp.int32, a.shape, 1)
    for d in (1, 2, 4):
        a_sh = jnp.where(sub >= d, pltpu.roll(a, d, 1), 1.0)
        b_sh = jnp.where(sub >= d, pltpu.roll(b, d, 1), 0.0)
        b = a * b_sh + b
        a = a * a_sh
    out = []
    for g in range(groups):
        hg = b[g] + a[g] * h0
        h0 = hg[SUBLANES - 1:SUBLANES]
        out.append(hg)
    return jnp.concatenate(out, axis=0), h0


def _in_proj_kernel(x_ref, g1_ref, w_ref, qg_ref, kg_ref, cw_ref, cb_ref, wri_ref, bri_ref, lam_ref,
                    ya_ref, q_ref, k_ref, v_ref, gate_ref, km_sc, xbuf, hst, hb_sc, y_sc, *, tiles_per_seq):
    tm = MOBA_BLOCK
    j = pl.program_id(0) % tiles_per_seq
    nt_dims = (((1,), (1,)), ((), ()))

    @pl.when(j == 0)
    def _():
        km_sc[...] = jnp.zeros_like(km_sc)
        xbuf[...] = jnp.zeros_like(xbuf)
        hst[...] = jnp.zeros_like(hst)

    hb_sc[...] = _rms(x_ref[...], g1_ref[...]).astype(BF16)
    pw = 4 * LANES

    y_sc[...] = jnp.dot(hb_sc[...], w_ref[...], preferred_element_type=F32)

    def proj(c0, width=pw):
        return y_sc[:, c0:c0 + width]

    xbuf[0:SUBLANES, :] = xbuf[tm:tm + SUBLANES, :]
    xbuf[SUBLANES:SUBLANES + tm, :] = proj(C_RNN, D_RNN)

    def rnn_block(n):
        cols = slice(n * RNN_BW, (n + 1) * RNN_BW)
        xa = cb_ref[:, cols]
        for k in range(CONV_W):
            off = SUBLANES - (CONV_W - 1) + k
            xa = xa + xbuf[off:off + tm, cols] * cw_ref[k:k + 1, cols]
        z = jnp.dot(xa.astype(BF16), wri_ref[n], preferred_element_type=F32) + bri_ref[n]
        r = jax.nn.sigmoid(z[:, 0:RNN_BW])
        gi = jax.nn.sigmoid(z[:, RNN_BW:2 * RNN_BW])
        a = jnp.exp(r * (-LRU_C * jax.nn.softplus(-lam_ref[:, cols])))
        w = 1.0 - a * a
        b = jnp.where(w > 0.0, w * lax.rsqrt(w), 0.0) * (gi * xa)
        hseq, hlast = _lru_scan(a, b, hst[0:1, cols])
        hst[0:1, cols] = hlast
        ya_ref[:, cols] = hseq.astype(BF16)
        ya_ref[:, D_RNN + n * RNN_BW:D_RNN + (n + 1) * RNN_BW] = proj(
            C_RNN + D_RNN + n * RNN_BW, RNN_BW).astype(BF16)

    nrow = km_sc.shape[0]
    blk = lax.broadcasted_iota(jnp.int32, (nrow, tm), 0)
    blk_f = blk.astype(F32)
    scale = HEAD_DIM ** -0.5 * LOG2_E

    def q_unit(p):
        yq = proj(C_Q + p * pw)
        for i in range(pw // HEAD_DIM):
            h = p * (pw // HEAD_DIM) + i
            cols = slice(h * HEAD_DIM, (h + 1) * HEAD_DIM)
            y = _rms(yq[:, i * HEAD_DIM:(i + 1) * HEAD_DIM], qg_ref[...]) * scale
            q_ref[0, h, 0:HEAD_DIM, :] = jnp.transpose(y).astype(BF16)
            km = km_sc[:, cols]
            km_hi = km.astype(BF16)
            km_lo = (km - km_hi.astype(F32)).astype(BF16)
            y_hi = y.astype(BF16)
            y_lo = (y - y_hi.astype(F32)).astype(BF16)
            gate_t = (lax.dot_general(km_hi, y_hi, nt_dims, preferred_element_type=F32)
                      + lax.dot_general(km_hi, y_lo, nt_dims, preferred_element_type=F32)
                      + lax.dot_general(km_lo, y_hi, nt_dims, preferred_element_type=F32))
            g = jnp.where(blk < j, gate_t, jnp.where(blk == j, TAKEN, NEG))
            for slot in range(MOBA_TOPK):
                gmax = jnp.max(g, axis=0, keepdims=True)
                idx = jnp.min(jnp.where(g == gmax, blk_f, float(LANES)), axis=0, keepdims=True)
                g = jnp.where(jnp.logical_and(blk_f == idx, slot < j), TAKEN, g)
            negsel_t = jnp.where(g == TAKEN, 0.0, NEG)
            if nrow < LANES:
                negsel_t = jnp.concatenate([negsel_t, jnp.zeros((LANES - nrow, tm), F32)], axis=0)
            q_ref[0, h, HEAD_DIM:2 * HEAD_DIM, :] = negsel_t.astype(BF16)

    lane = lax.broadcasted_iota(jnp.int32, (tm, LANES), 1)
    onehot = jnp.where(lane == j, 1.0, 0.0).astype(BF16)

    def k_unit(p):
        yk = proj(C_K + p * pw)
        for i in range(pw // HEAD_DIM):
            h = p * (pw // HEAD_DIM) + i
            cols = slice(h * HEAD_DIM, (h + 1) * HEAD_DIM)
            y = _rms(yk[:, i * HEAD_DIM:(i + 1) * HEAD_DIM], kg_ref[...])
            k_ref[0, h, :, 0:HEAD_DIM] = y.astype(BF16)
            k_ref[0, h, :, HEAD_DIM:2 * HEAD_DIM] = onehot
            km_sc[:, cols] = jnp.where(blk[:, 0:HEAD_DIM] == j, jnp.mean(y, axis=0, keepdims=True),
                                       km_sc[:, cols])

    ones = jnp.ones((V_ROWS - HEAD_DIM, tm), BF16)

    def v_unit(p):
        yv = proj(C_V + p * pw)
        for i in range(pw // HEAD_DIM):
            h = p * (pw // HEAD_DIM) + i
            v_ref[0, h, 0:HEAD_DIM, :] = jnp.transpose(yv[:, i * HEAD_DIM:(i + 1) * HEAD_DIM]).astype(BF16)
            v_ref[0, h, HEAD_DIM:V_ROWS, :] = ones

    def gate_unit(c):
        gate_ref[:, c * pw:(c + 1) * pw] = proj(C_G + c * pw).astype(BF16)

    units = ([functools.partial(q_unit, p) for p in range(D_ATTN // pw)]
             + [functools.partial(k_unit, p) for p in range(D_ATTN // pw)]
             + [functools.partial(v_unit, p) for p in range(D_ATTN // pw)]
             + [functools.partial(gate_unit, c) for c in range(2 * D_MODEL // pw)])
    for n in range(RNN_BLOCKS):
        rnn_block(n)
        if n < len(units):
            units[n]()
    for unit in units[RNN_BLOCKS:]:
        unit()


def _in_proj(x2, g1, w_in, qg, kg, cw, cb, wri, bri, lam, *, bsz, seq):
    t = bsz * seq
    tm = MOBA_BLOCK
    nst = seq // tm
    nrow = -(-nst // SUBLANES) * SUBLANES
    kern = functools.partial(_in_proj_kernel, tiles_per_seq=nst)
    return pl.pallas_call(
        kern,
        grid=(t // tm,),
        in_specs=[
            pl.BlockSpec((tm, D_MODEL), lambda i: (i, 0)),
            _const_spec((1, D_MODEL)),
            _const_spec((D_MODEL, D_IN)),
            _const_spec((1, HEAD_DIM)),
            _const_spec((1, HEAD_DIM)),
            _const_spec((CONV_W, D_RNN)),
            _const_spec((1, D_RNN)),
            _const_spec((RNN_BLOCKS, RNN_BW, 2 * RNN_BW)),
            _const_spec((RNN_BLOCKS, 1, 2 * RNN_BW)),
            _const_spec((1, D_RNN)),
        ],
        out_specs=[
            pl.BlockSpec((tm, 2 * D_RNN), lambda i: (i, 0)),
            pl.BlockSpec((1, N_HEADS, 2 * HEAD_DIM, tm), lambda i: (i // nst, 0, 0, i % nst)),
            pl.BlockSpec((1, N_HEADS, tm, 2 * HEAD_DIM), lambda i: (i // nst, 0, i % nst, 0)),
            pl.BlockSpec((1, N_HEADS, V_ROWS, tm), lambda i: (i // nst, 0, 0, i % nst)),
            pl.BlockSpec((tm, 2 * D_MODEL), lambda i: (i, 0)),
        ],
        out_shape=[
            jax.ShapeDtypeStruct((t, 2 * D_RNN), BF16),
            jax.ShapeDtypeStruct((bsz, N_HEADS, 2 * HEAD_DIM, seq), BF16),
            jax.ShapeDtypeStruct((bsz, N_HEADS, seq, 2 * HEAD_DIM), BF16),
            jax.ShapeDtypeStruct((bsz, N_HEADS, V_ROWS, seq), BF16),
            jax.ShapeDtypeStruct((t, 2 * D_MODEL), BF16),
        ],
        scratch_shapes=[
            pltpu.VMEM((nrow, D_ATTN), F32),
            pltpu.VMEM((tm + SUBLANES, D_RNN), F32),
            pltpu.VMEM((SUBLANES, D_RNN), F32),
            pltpu.VMEM((tm, D_MODEL), BF16),
            pltpu.VMEM((tm, D_IN), F32),
        ],
        compiler_params=pltpu.CompilerParams(
            dimension_semantics=("arbitrary",), vmem_limit_bytes=VMEM_LIMIT),
        name="in_proj",
    )(x2, g1, w_in, qg, kg, cw, cb, wri, bri, lam)


def _moba_kernel(q_ref, qn_ref, k_ref, v_ref, o_ref, sa_sc, sb_sc, xa_sc, xb_sc, m_sc, acc_sc, *, hp):
    t = pl.program_id(2)
    kc = MOBA_TILE
    heads = range(hp)

    def scores(h, chunk, dst, q=q_ref):
        st = pl.multiple_of(chunk * kc, kc)
        sc = jnp.dot(k_ref[0, h, pl.ds(st, kc), :], q[0, h], preferred_element_type=F32)
        dst[0][h] = sc
        dst[1][h] = jnp.max(sc, axis=0, keepdims=True)

    def accumulate(h, chunk, src, causal=False):
        st = pl.multiple_of(chunk * kc, kc)
        sc = src[0][h]
        if causal:
            key_i = lax.broadcasted_iota(jnp.int32, sc.shape, 0)
            qry_i = lax.broadcasted_iota(jnp.int32, sc.shape, 1)
            sc = jnp.where(key_i <= qry_i, sc, NEG)
        m_prev = m_sc[h]
        m_new = jnp.maximum(m_prev, src[1][h])
        p = jnp.exp2(sc - m_new).astype(BF16)
        acc_sc[h] = jnp.exp2(m_prev - m_new) * acc_sc[h] + jnp.dot(
            v_ref[0, h, :, pl.ds(st, kc)], p, preferred_element_type=F32)
        m_sc[h] = m_new

    m_sc[...] = jnp.full_like(m_sc, NEG)
    acc_sc[...] = jnp.zeros_like(acc_sc)
    buf_a, buf_b = (sa_sc, xa_sc), (sb_sc, xb_sc)

    @pl.when(t == 0)
    def _():
        for h in heads:
            scores(h, 0, buf_a)

    def step(c, cur, nxt):
        for h in heads:
            scores(h, c + 1, nxt)
        for h in heads:
            accumulate(h, c, cur)

    def run(first, trips, unroll):
        def body(i, carry):
            for u in range(unroll):
                cur, nxt = (buf_a, buf_b) if u % 2 == 0 else (buf_b, buf_a)
                step(first + unroll * i + u, cur, nxt)
            return carry

        lax.fori_loop(0, trips, body, 0)

    trips = lax.div(t, LONG_BODY)
    done = trips * LONG_BODY
    rest = t - done
    run(0, trips, LONG_BODY)

    def finish(own_buf):
        for h in heads:
            accumulate(h, t, own_buf, causal=True)
        for h in heads:
            scores(h, 0, buf_a, qn_ref)
        for h in heads:
            acc = acc_sc[h]
            o_t = acc[0:HEAD_DIM, :] / acc[HEAD_DIM:HEAD_DIM + 1, :]
            o_ref[0, :, h * HEAD_DIM:(h + 1) * HEAD_DIM] = jnp.transpose(o_t).astype(BF16)

    for left in range(LONG_BODY):
        @pl.when(rest == left)
        def _(left=left):
            for u in range(left):
                cur, nxt = (buf_a, buf_b) if u % 2 == 0 else (buf_b, buf_a)
                step(done + u, cur, nxt)
            finish(buf_a if left % 2 == 0 else buf_b)


def _moba(q, k, v, *, hp):
    bsz, nh, _, seq = q.shape
    assert seq % MOBA_TILE == 0 and nh % hp == 0
    kc = MOBA_TILE
    last = seq // MOBA_TILE - 1
    kern = functools.partial(_moba_kernel, hp=hp)
    return pl.pallas_call(
        kern,
        grid=(bsz, nh // hp, seq // MOBA_TILE),
        in_specs=[
            pl.BlockSpec((1, hp, 2 * HEAD_DIM, MOBA_TILE), lambda b, h, j: (b, h, 0, j)),
            pl.BlockSpec((1, hp, 2 * HEAD_DIM, MOBA_TILE), lambda b, h, j: (b, h, 0, jnp.minimum(j + 1, last))),
            pl.BlockSpec((1, hp, seq, 2 * HEAD_DIM), lambda b, h, j: (b, h, 0, 0)),
            pl.BlockSpec((1, hp, V_ROWS, seq), lambda b, h, j: (b, h, 0, 0)),
        ],
        out_specs=pl.BlockSpec((1, MOBA_TILE, hp * HEAD_DIM), lambda b, h, j: (b, j, h)),
        out_shape=jax.ShapeDtypeStruct((bsz, seq, D_ATTN), BF16),
        scratch_shapes=[
            pltpu.VMEM((hp, kc, MOBA_TILE), F32),
            pltpu.VMEM((hp, kc, MOBA_TILE), F32),
            pltpu.VMEM((hp, 1, MOBA_TILE), F32),
            pltpu.VMEM((hp, 1, MOBA_TILE), F32),
            pltpu.VMEM((hp, 1, MOBA_TILE), F32),
            pltpu.VMEM((hp, V_ROWS, MOBA_TILE), F32),
        ],
        compiler_params=pltpu.CompilerParams(
            dimension_semantics=("arbitrary", "arbitrary", "arbitrary"), vmem_limit_bytes=VMEM_LIMIT),
        name="moba",
    )(q, q, k, v)


def _merge_kernel(ya_ref, yb_ref, gate_ref, x_ref, pa_ref, pb_ref, wo_ref, o_ref):
    ya = (ya_ref[:, 0:D_RNN].astype(F32) * _gelu(ya_ref[:, D_RNN:2 * D_RNN].astype(F32))).astype(BF16)
    a = jnp.dot(ya, pa_ref[...], preferred_element_type=F32)
    b = jnp.dot(yb_ref[...], pb_ref[...], preferred_element_type=F32)
    merged = (jax.nn.sigmoid(gate_ref[:, 0:D_MODEL].astype(F32)) * a
              + jax.nn.sigmoid(gate_ref[:, D_MODEL:2 * D_MODEL].astype(F32)) * b)
    o_ref[...] = x_ref[...] + jnp.dot(merged.astype(BF16), wo_ref[...], preferred_element_type=F32)


def _merge(ya, yb, gates, x2, pa, pb, wo, *, tm):
    t = x2.shape[0]
    row = lambda w: pl.BlockSpec((tm, w), lambda i: (i, 0))
    return pl.pallas_call(
        _merge_kernel,
        grid=(t // tm,),
        in_specs=[row(2 * D_RNN), row(D_ATTN), row(2 * D_MODEL), row(D_MODEL),
                  _const_spec((D_RNN, D_MODEL)), _const_spec((D_ATTN, D_MODEL)),
                  _const_spec((D_MODEL, D_MODEL))],
        out_specs=row(D_MODEL),
        out_shape=jax.ShapeDtypeStruct((t, D_MODEL), F32),
        compiler_params=pltpu.CompilerParams(
            dimension_semantics=("arbitrary",), vmem_limit_bytes=VMEM_LIMIT),
        name="merge",
    )(ya, yb, gates, x2, pa, pb, wo)


def _ffn_kernel(x_ref, g2_ref, wu_ref, wg_ref, cw_ref, cb_ref, wd_ref, o_ref, ubuf, *, tm):
    s = pl.program_id(1)

    @pl.when(s == 0)
    def _():
        ubuf[...] = jnp.zeros_like(ubuf)

    x = x_ref[0]
    hb = _rms(x, g2_ref[...]).astype(BF16)
    ubuf[0:SUBLANES, :] = ubuf[tm:tm + SUBLANES, :]
    ubuf[SUBLANES:SUBLANES + tm, :] = jnp.dot(hb, wu_ref[...], preferred_element_type=F32)
    conv = cb_ref[...]
    for k in range(FFN_CONV_W):
        off = SUBLANES - (FFN_CONV_W - 1) + k
        conv = conv + ubuf[off:off + tm, :] * cw_ref[k:k + 1, :]
    act = _gelu(conv) * jnp.dot(hb, wg_ref[...], preferred_element_type=F32)
    o_ref[0] = x + jnp.dot(act.astype(BF16), wd_ref[...], preferred_element_type=F32)


def _ffn(x3, g2, wu, wg, cw, cb, wd, *, tm):
    bsz, seq, _ = x3.shape
    kern = functools.partial(_ffn_kernel, tm=tm)
    return pl.pallas_call(
        kern,
        grid=(bsz, seq // tm),
        in_specs=[
            pl.BlockSpec((1, tm, D_MODEL), lambda b, s: (b, s, 0)),
            _const_spec((1, D_MODEL)),
            _const_spec((D_MODEL, D_FF)),
            _const_spec((D_MODEL, D_FF)),
            _const_spec((FFN_CONV_W, D_FF)),
            _const_spec((1, D_FF)),
            _const_spec((D_FF, D_MODEL)),
        ],
        out_specs=pl.BlockSpec((1, tm, D_MODEL), lambda b, s: (b, s, 0)),
        out_shape=jax.ShapeDtypeStruct((bsz, seq, D_MODEL), F32),
        scratch_shapes=[pltpu.VMEM((tm + SUBLANES, D_FF), F32)],
        compiler_params=pltpu.CompilerParams(
            dimension_semantics=("arbitrary", "arbitrary"), vmem_limit_bytes=VMEM_LIMIT),
        name="ffn",
    )(x3, g2, wu, wg, cw, cb, wd)


def _tile(seq, want):
    t = min(want, seq)
    assert seq % t == 0 and t % MOBA_BLOCK == 0
    return t


def kernel(x, norm1_g, w_in, conv_w, conv_b, w_r, b_r, w_i, b_i, lru_lambda, q_norm_g, k_norm_g,
           w_proj_rnn, w_proj_attn, w_out, norm2_g, w_up, w_gate, ffn_conv_w, ffn_conv_b, w_down):
    bsz, seq, d = x.shape
    assert d == D_MODEL and seq % MOBA_BLOCK == 0 and seq // MOBA_BLOCK <= LANES
    depth = norm1_g.shape[0]
    t = bsz * seq
    for l in range(depth):
        x2 = x.reshape(t, D_MODEL)
        wri = jnp.concatenate([w_r[l], w_i[l]], axis=-1).astype(BF16)
        bri = jnp.concatenate([b_r[l], b_i[l]], axis=-1)[:, None, :]
        ya, q, k, v, gates = _in_proj(
            x2, norm1_g[l][None], w_in[l].astype(BF16), q_norm_g[l][None], k_norm_g[l][None],
            conv_w[l], conv_b[l][None], wri, bri, lru_lambda[l][None], bsz=bsz, seq=seq)
        yb = _moba(q, k, v, hp=2)
        x1 = _merge(ya, yb.reshape(t, D_ATTN), gates, x2,
                    w_proj_rnn[l].astype(BF16), w_proj_attn[l].astype(BF16), w_out[l].astype(BF16),
                    tm=_tile(seq, 512))
        x = _ffn(x1.reshape(bsz, seq, D_MODEL), norm2_g[l][None], w_up[l].astype(BF16),
                 w_gate[l].astype(BF16), ffn_conv_w[l], ffn_conv_b[l][None], w_down[l].astype(BF16),
                 tm=_tile(seq, 512))
    return x
```
